```python
import jax
import jax.numpy as jnp
from jax import lax
import numpy as np

D_MODEL = 4096
BATCH = 4
SEQ = 2048
DEPTH = 2
DEC_BATCH = 8
DEC_SEQ = 4
PAST_LEN = 16384
PAGE_SIZE = 128

A_PATTERNS = ((128, 1), (512, 4), (2048, 16))
N_GROUPS_A = len(A_PATTERNS)
H_A = 8
HD_A = 128
A_OUT = H_A * HD_A
ROT_A = HD_A // 4
ROPE_THETA = 500000.0
DA_BLOCK = 128
A_COLS = N_GROUPS_A * 3 * A_OUT
H_B = 4
DK_B = 128
DV_B = 256
B_OUT = H_B * DV_B
B_COLS = 2 * H_B * DK_B + 2 * B_OUT
RET_THETA = 10000.0
RET_CHUNK = 128
H_C = 16
HD_C = 64
C_W = H_C * HD_C
LORA_W = 64
LORA_A = 64
LORA_G = 160
C_COLS = 3 * C_W + LORA_W + LORA_A + LORA_G
WKV_GN_EPS = 64e-5
N_BRANCH = 3
IN_COLS = A_COLS + B_COLS + C_COLS + N_BRANCH * D_MODEL
N_EXPERTS = 128
N_EXP_GROUPS = 8
TOPK_GROUPS = 4
TOP_K = 8
D_EXPERT = 768
D_SHARED = 768
ROUTED_SCALE = 2.5
MOE_BLOCK = 128
NORM_EPS = 1e-6

kernel_name = 'hybrid_dilated_retention_rwkv7_moe_decode_step'


def rms_norm(x, g):
    x32 = x.astype(jnp.float32)
    y = x32 * lax.rsqrt(jnp.mean(x32 * x32, axis=-1, keepdims=True) + NORM_EPS)
    return (y * g.astype(jnp.float32)).astype(x.dtype)


def rope(x, pos, rot_dim, theta):
    half = rot_dim // 2
    inv = theta ** (-jnp.arange(half, dtype=jnp.float32) / half)
    ang = pos.astype(jnp.float32)[:, None] * inv[None, :]
    cos = jnp.cos(ang)[None, :, None, :]
    sin = jnp.sin(ang)[None, :, None, :]
    x32 = x.astype(jnp.float32)
    x1 = x32[..., :half]
    x2 = x32[..., half:rot_dim]
    out = jnp.concatenate([x1 * cos - x2 * sin, x2 * cos + x1 * sin, x32[..., rot_dim:]], axis=-1)
    return out.astype(x.dtype)


def dilated_band_prompt(q, k, v, dil, n_back):
    B, T, H, E = q.shape
    blk = DA_BLOCK
    unit = dil * blk
    t_pad = -(-T // unit) * unit
    nb = t_pad // unit

    def blocks(z):
        z = jnp.pad(z, ((0, 0), (0, t_pad - T), (0, 0), (0, 0)))
        return z.reshape(B, nb, blk, dil, H, E)

    def with_prev(z):
        prev = jnp.pad(z, ((0, 0), (1, 0), (0, 0), (0, 0), (0, 0), (0, 0)))[:, :-1]
        return jnp.concatenate([prev, z], axis=2)

    qb = blocks(q)
    kk = with_prev(blocks(k))
    vv = with_prev(blocks(v))
    logits = jnp.einsum('bnqrhe,bnkrhe->bnrhqk', qb, kk).astype(jnp.float32) * (E ** -0.5)
    qi = jnp.arange(blk)[:, None]
    ki = jnp.arange(2 * blk)[None, :]
    dist = blk + qi - ki
    band = (dist >= 0) & (dist <= n_back)
    real = (jnp.arange(nb) > 0)[:, None, None] | (ki >= blk)[None]
    mask = band[None] & real
    logits = jnp.where(mask[None, :, None, None], logits, -jnp.inf)
    m = jnp.max(logits, axis=-1)
    pr = jnp.exp(logits - m[..., None])
    s = jnp.sum(pr, axis=-1)
    o = jnp.einsum('bnrhqk,bnkrhe->bnqrhe', pr, vv.astype(jnp.float32))

    def to_seq(z):
        return jnp.transpose(z, (0, 1, 4, 2, 3)).reshape(B, t_pad, H)[:, :T]

    m = to_seq(m)
    s = to_seq(s)
    o = o.reshape(B, t_pad, H, E)[:, :T] / s[..., None]
    return o, m, s


def dilated_window_sample(q, k, v, buf, dil, n_back):
    B, S, H, E = q.shape
    lb = buf.shape[1]
    keys = jnp.concatenate([buf[:, :, 0].astype(k.dtype), k], axis=1)
    vals = jnp.concatenate([buf[:, :, 1].astype(v.dtype), v], axis=1)
    idx = lb + jnp.arange(S)[:, None] - dil * jnp.arange(n_back + 1)[None, :]
    valid = idx >= 0
    idx = jnp.maximum(idx, 0)
    kg = keys[:, idx]
    vg = vals[:, idx]
    logits = jnp.einsum('bshe,bsjhe->bshj', q, kg).astype(jnp.float32) * (E ** -0.5)
    logits = jnp.where(valid[None, :, None, :], logits, -jnp.inf)
    m = jnp.max(logits, axis=-1)
    pr = jnp.exp(logits - m[..., None])
    s = jnp.sum(pr, axis=-1)
    o = jnp.einsum('bshj,bsjhe->bshe', pr, vg.astype(jnp.float32)) / s[..., None]
    return o, m, s


def merge_by_denominator(outs, maxes, denoms):
    m = jnp.stack(maxes)
    s = jnp.stack(denoms)
    o = jnp.stack(outs)
    wgt = s * jnp.exp(m - jnp.max(m, axis=0, keepdims=True))
    return jnp.sum(wgt[..., None] * o, axis=0) / jnp.sum(wgt, axis=0)[..., None]


def dilated_branch(za, pos, a_caches):
    B, T, _ = za.shape
    za = za.reshape(B, T, N_GROUPS_A, 3, H_A, HD_A)
    outs, maxes, denoms, rows = [], [], [], []
    for g, (win, dil) in enumerate(A_PATTERNS):
        q = rope(za[:, :, g, 0], pos, ROT_A, ROPE_THETA)
        k = rope(za[:, :, g, 1], pos, ROT_A, ROPE_THETA)
        v = za[:, :, g, 2]
        kv = jnp.stack([k, v], axis=2)
        if a_caches is None:
            o, m, s = dilated_band_prompt(q, k, v, dil, win // dil)
            rows.append(kv[:, T - min(win, T):])
        else:
            o, m, s = dilated_window_sample(q, k, v, a_caches[g], dil, win // dil)
            rows.append(kv)
        outs.append(o)
        maxes.append(m)
        denoms.append(s)
    o = merge_by_denominator(outs, maxes, denoms)
    return o.reshape(B, T, A_OUT).astype(za.dtype), rows


def retention_chunk(S, q, k, v, log_gamma):
    L = q.shape[1]
    i = jnp.arange(L, dtype=jnp.float32)
    diff = i[:, None] - i[None, :]
    inner = jnp.where(diff >= 0, jnp.exp(jnp.maximum(diff, 0.0)[None] * log_gamma[:, None, None]), 0.0)
    scores = jnp.einsum('bqhd,bkhd->bhqk', q, k) * inner[None]
    o = jnp.einsum('bhqk,bkhe->bqhe', scores, v)
    o = o + jnp.einsum('bqhd,bhde->bqhe', q, S) * jnp.exp((i + 1.0)[:, None] * log_gamma[None, :])[None, :, :, None]
    k_dec = k * jnp.exp((L - 1.0 - i)[:, None] * log_gamma[None, :])[None, :, :, None]
    S = S * jnp.exp(L * log_gamma)[None, :, None, None] + jnp.einsum('bkhd,bkhe->bhde', k_dec, v)
    return S, o


def retention_branch(zb, pos, S0):
    B, T, _ = zb.shape
    q, k, v, g = jnp.split(zb, [H_B * DK_B, 2 * H_B * DK_B, 2 * H_B * DK_B + B_OUT], axis=-1)
    q = rope(q.reshape(B, T, H_B, DK_B), pos, DK_B, RET_THETA).astype(jnp.float32)
    k = rope(k.reshape(B, T, H_B, DK_B), pos, DK_B, RET_THETA).astype(jnp.float32) * (DK_B ** -0.5)
    v = v.reshape(B, T, H_B, DV_B).astype(jnp.float32)
    log_gamma = jnp.log(1.0 - 2.0 ** (-5.0 - jnp.arange(H_B, dtype=jnp.float32)))
    chunk = RET_CHUNK if T % RET_CHUNK == 0 else T
    nc = T // chunk

    def to_chunks(z):
        return jnp.moveaxis(z.reshape(B, nc, chunk, z.shape[2], z.shape[3]), 1, 0)

    S, o = lax.scan(lambda St, c: retention_chunk(St, c[0], c[1], c[2], log_gamma),
                    S0.astype(jnp.float32), (to_chunks(q), to_chunks(k), to_chunks(v)))
    o = jnp.moveaxis(o, 0, 1).reshape(B, T, H_B, DV_B)
    o = o * lax.rsqrt(jnp.mean(o * o, axis=-1, keepdims=True) + NORM_EPS)
    o = jax.nn.silu(g.astype(jnp.float32)) * o.reshape(B, T, B_OUT)
    return o.astype(zb.dtype), S


def wkv7_scan(S0, r, w, k, v, a, b):
    def step(S, inp):
        r_t, w_t, k_t, v_t, a_t, b_t = inp
        sa = jnp.einsum('bhij,bhj->bhi', S, a_t)
        S = S * w_t[:, :, None, :] + sa[..., None] * b_t[:, :, None, :] + v_t[..., None] * k_t[:, :, None, :]
        return S, jnp.einsum('bhij,bhj->bhi', S, r_t)

    xs = tuple(jnp.moveaxis(z, 1, 0) for z in (r, w, k, v, a, b))
    S, y = lax.scan(step, S0, xs)
    return jnp.moveaxis(y, 0, 1), S


def rwkv7_branch(zc, prev_row, S0, mu, w0, w_dec_up, a0, a_up, g_up, k_k, k_a, r_k, lnx_w, lnx_b):
    B, T, _ = zc.shape
    prev = jnp.concatenate([prev_row.astype(zc.dtype)[:, None], zc[:, :-1]], axis=1)
    xc = (zc + (prev - zc) * mu).astype(jnp.float32)
    r, k, v, xw, xa, xg = jnp.split(
        xc, [C_W, 2 * C_W, 3 * C_W, 3 * C_W + LORA_W, 3 * C_W + LORA_W + LORA_A], axis=-1)
    w = -jax.nn.softplus(-(w0 + jnp.tanh(xw) @ w_dec_up)) - 0.5
    decay = jnp.exp(-jnp.exp(w))
    a = jax.nn.sigmoid(a0 + xa @ a_up)
    g = jax.nn.sigmoid(xg) @ g_up

    def heads(z):
        return z.reshape(B, T, H_C, HD_C)

    kk = heads(k * k_k)
    kk = kk / jnp.maximum(jnp.linalg.norm(kk, axis=-1, keepdims=True), 1e-12)
    k = k * (1.0 + (a - 1.0) * k_a)
    r_h, k_h, v_h, a_h = heads(r), heads(k), heads(v), heads(a)
    y, S = wkv7_scan(S0.astype(jnp.float32), r_h, heads(decay), k_h, v_h, -kk, kk * a_h)
    mean = jnp.mean(y, axis=-1, keepdims=True)
    var = jnp.mean(jnp.square(y - mean), axis=-1, keepdims=True)
    y = ((y - mean) * lax.rsqrt(var + WKV_GN_EPS)).reshape(B, T, C_W) * lnx_w + lnx_b
    y = y + (jnp.sum(r_h * k_h * r_k, axis=-1, keepdims=True) * v_h).reshape(B, T, C_W)
    return (y * g).astype(zc.dtype), S


def routed_experts(hf, idx, wts, w_exp_gu, w_exp_down, l):
    n, D = hf.shape
    n_assign = n * TOP_K
    bm = max(8, min(MOE_BLOCK, n_assign // N_EXPERTS))
    n_blocks = -(-n_assign // bm) + N_EXPERTS
    flat_e = idx.reshape(n_assign)
    order = jnp.argsort(flat_e)
    sorted_e = flat_e[order]
    counts = jnp.zeros((N_EXPERTS,), jnp.int32).at[flat_e].add(1)
    padded = (counts + bm - 1) // bm * bm
    pad_end = jnp.cumsum(padded)
    slot_sorted = ((pad_end - padded)[sorted_e] + jnp.arange(n_assign, dtype=jnp.int32)
                   - (jnp.cumsum(counts) - counts)[sorted_e])
    slot = jnp.zeros((n_assign,), jnp.int32).at[order].set(slot_sorted)
    tok = jnp.full((n_blocks * bm,), n, jnp.int32).at[slot].set(jnp.arange(n_assign, dtype=jnp.int32) // TOP_K)
    wslot = jnp.zeros((n_blocks * bm,), jnp.float32).at[slot].set(wts.reshape(n_assign))
    blk_e = jnp.minimum(jnp.searchsorted(pad_end, jnp.arange(n_blocks, dtype=jnp.int32) * bm, side='right'),
                        N_EXPERTS - 1)
    h_pad = jnp.concatenate([hf, jnp.zeros((1, D), hf.dtype)], axis=0)

    def expert_block(acc, blk):
        tb, wb, e = blk
        gate, up = jnp.split(h_pad[tb] @ w_exp_gu[l, e], 2, axis=-1)
        ye = (jax.nn.silu(gate) * up) @ w_exp_down[l, e]
        return acc.at[tb].add(ye.astype(jnp.float32) * wb[:, None]), None

    acc, _ = lax.scan(expert_block, jnp.zeros((n + 1, D), jnp.float32),
                      (tok.reshape(n_blocks, bm), wslot.reshape(n_blocks, bm), blk_e))
    return acc[:n]


def moe_ffn(h, w_router, b_router, w_exp_gu, w_exp_down, l, w_sh_gu, w_sh_down):
    B, T, D = h.shape
    n = B * T
    hf = h.reshape(n, D)
    scores = jax.nn.sigmoid((hf @ w_router).astype(jnp.float32))
    choice = scores + b_router.astype(jnp.float32)
    per_group = N_EXPERTS // N_EXP_GROUPS
    group_score = jnp.sum(lax.top_k(choice.reshape(n, N_EXP_GROUPS, per_group), 2)[0], axis=-1)
    top_groups = lax.top_k(group_score, TOPK_GROUPS)[1]
    keep = jnp.sum(jax.nn.one_hot(top_groups, N_EXP_GROUPS, dtype=jnp.float32), axis=1) > 0
    masked = jnp.where(jnp.repeat(keep, per_group, axis=1), choice, -jnp.inf)
    idx = lax.top_k(masked, TOP_K)[1]
    wts = jnp.take_along_axis(scores, idx, axis=1)
    wts = wts / jnp.sum(wts, axis=-1, keepdims=True) * ROUTED_SCALE
    routed = routed_experts(hf, idx, wts, w_exp_gu, w_exp_down, l)
    gate, up = jnp.split(hf @ w_sh_gu, 2, axis=-1)
    shared = (jax.nn.silu(gate) * up) @ w_sh_down
    return (routed.astype(h.dtype) + shared).reshape(B, T, D)


def layer_forward(x, c, pos, a_caches, ret_S0, wkv_S0, shift_row, p, w_exp_gu, w_exp_down, l):
    B, T, D = x.shape
    mod = jax.nn.silu(c) @ p['w_ada'] + p['b_ada']
    sh1, sc1, gt1, sh2, sc2, gt2 = [m[:, None, :] for m in jnp.split(mod, 6, axis=-1)]
    h = rms_norm(x, p['norm1']) * (1.0 + sc1) + sh1
    z = h @ p['w_in']
    za, zb, zc, zg = jnp.split(z, [A_COLS, A_COLS + B_COLS, A_COLS + B_COLS + C_COLS], axis=-1)
    oa, a_rows = dilated_branch(za, pos, a_caches)
    ob, ret_S = retention_branch(zb, pos, ret_S0)
    oc, wkv_S = rwkv7_branch(zc, shift_row, wkv_S0, p['mu_c'], p['w_decay0'], p['w_decay_up'], p['a0'],
                             p['a_up'], p['g_up'], p['k_k'], p['k_a'], p['r_k'], p['lnx_w'], p['lnx_b'])
    gates = jax.nn.sigmoid(zg.reshape(B, T, N_BRANCH, D))
    merged = (gates[:, :, 0] * (oa @ p['w_up_a']) + gates[:, :, 1] * (ob @ p['w_up_b'])
              + gates[:, :, 2] * (oc @ p['w_up_c']))
    x = x + gt1 * (merged @ p['w_out'])
    h = rms_norm(x, p['norm2']) * (1.0 + sc2) + sh2
    x = x + gt2 * moe_ffn(h, p['w_router'], p['b_router'], w_exp_gu, w_exp_down, l, p['w_sh_gu'], p['w_sh_down'])
    return x, (a_rows[0], a_rows[1], a_rows[2], ret_S, wkv_S, zc[:, -1])


def setup_inputs(seed: int = 0) -> dict:
    key = jax.random.key(seed)
    ks = iter(jax.random.split(key, 48))

    def nrm(shape, scale):
        return jax.random.normal(next(ks), shape, jnp.float32) * scale

    def unif(shape, lo, hi):
        return jax.random.uniform(next(ks), shape, jnp.float32, lo, hi)

    L, D = DEPTH, D_MODEL
    win_rows = [min(w, PAST_LEN) for w, _ in A_PATTERNS]
    return {
        'x_prompt': nrm((BATCH, SEQ, D), 1.0),
        'x_sample': nrm((DEC_BATCH, DEC_SEQ, D), 1.0),
        'c_prompt': nrm((BATCH, D), 1.0),
        'c_sample': nrm((DEC_BATCH, D), 1.0),
        'cache_a0': nrm((L, DEC_BATCH, win_rows[0], 2, H_A, HD_A), 1.0),
        'cache_a1': nrm((L, DEC_BATCH, win_rows[1], 2, H_A, HD_A), 1.0),
        'cache_a2': nrm((L, DEC_BATCH, win_rows[2], 2, H_A, HD_A), 1.0),
        'state_ret': nrm((L, DEC_BATCH, H_B, DK_B, DV_B), 0.5),
        'state_wkv': nrm((L, DEC_BATCH, H_C, HD_C, HD_C), 0.3),
        'state_shift': nrm((L, DEC_BATCH, C_COLS), 1.0),
        'w_ada': nrm((L, D, 6 * D), 0.5 * D ** -0.5),
        'b_ada': nrm((L, 6 * D), 0.02),
        'norm1': 1.0 + nrm((L, D), 0.02),
        'norm2': 1.0 + nrm((L, D), 0.02),
        'w_in': nrm((L, D, IN_COLS), D ** -0.5),
        'mu_c': unif((L, C_COLS), 0.0, 1.0),
        'w_decay0': unif((L, C_W), -6.0, -1.0),
        'w_decay_up': nrm((L, LORA_W, C_W), 0.1),
        'a0': nrm((L, C_W), 0.2),
        'a_up': nrm((L, LORA_A, C_W), 0.1),
        'g_up': nrm((L, LORA_G, C_W), LORA_G ** -0.5),
        'k_k': 0.85 + nrm((L, C_W), 0.05),
        'k_a': 1.0 + nrm((L, C_W), 0.05),
        'r_k': nrm((L, H_C, HD_C), 0.1),
        'lnx_w': 1.0 + nrm((L, C_W), 0.02),
        'lnx_b': nrm((L, C_W), 0.02),
        'w_up_a': nrm((L, A_OUT, D), A_OUT ** -0.5),
        'w_up_b': nrm((L, B_OUT, D), B_OUT ** -0.5),
        'w_up_c': nrm((L, C_W, D), C_W ** -0.5),
        'w_out': nrm((L, D, D), D ** -0.5),
        'w_router': nrm((L, D, N_EXPERTS), D ** -0.5),
        'b_router': nrm((L, N_EXPERTS), 0.01),
        'w_exp_gu': nrm((L, N_EXPERTS, D, 2 * D_EXPERT), D ** -0.5),
        'w_exp_down': nrm((L, N_EXPERTS, D_EXPERT, D), D_EXPERT ** -0.5),
        'w_sh_gu': nrm((L, D, 2 * D_SHARED), D ** -0.5),
        'w_sh_down': nrm((L, D_SHARED, D), D_SHARED ** -0.5),
        'norm_f': 1.0 + nrm((D,), 0.02),
    }


def reference(x_prompt, x_sample, c_prompt, c_sample, cache_a0, cache_a1, cache_a2, state_ret, state_wkv,
              state_shift, w_ada, b_ada, norm1, norm2, w_in, mu_c, w_decay0, w_decay_up, a0, a_up, g_up, k_k,
              k_a, r_k, lnx_w, lnx_b, w_up_a, w_up_b, w_up_c, w_out, w_router, b_router, w_exp_gu, w_exp_down,
              w_sh_gu, w_sh_down, norm_f):
    bp, tp = x_prompt.shape[0], x_prompt.shape[1]
    pos_p = jnp.arange(tp, dtype=jnp.int32)
    pos_s = PAST_LEN + jnp.arange(x_sample.shape[1], dtype=jnp.int32)
    xp, xs = x_prompt, x_sample
    new_p = [[] for _ in range(6)]
    new_s = [[] for _ in range(6)]
    for l in range(DEPTH):
        p = {'w_ada': w_ada[l], 'b_ada': b_ada[l], 'norm1': norm1[l], 'norm2': norm2[l], 'w_in': w_in[l],
             'mu_c': mu_c[l], 'w_decay0': w_decay0[l], 'w_decay_up': w_decay_up[l], 'a0': a0[l],
             'a_up': a_up[l], 'g_up': g_up[l], 'k_k': k_k[l], 'k_a': k_a[l], 'r_k': r_k[l],
             'lnx_w': lnx_w[l], 'lnx_b': lnx_b[l], 'w_up_a': w_up_a[l], 'w_up_b': w_up_b[l],
             'w_up_c': w_up_c[l], 'w_out': w_out[l], 'w_router': w_router[l], 'b_router': b_router[l],
             'w_sh_gu': w_sh_gu[l], 'w_sh_down': w_sh_down[l]}
        xp, st_p = layer_forward(xp, c_prompt, pos_p, None,
                                 jnp.zeros((bp, H_B, DK_B, DV_B), jnp.float32),
                                 jnp.zeros((bp, H_C, HD_C, HD_C), jnp.float32),
                                 jnp.zeros((bp, C_COLS), xp.dtype), p, w_exp_gu, w_exp_down, l)
        xs, st_s = layer_forward(xs, c_sample, pos_s, (cache_a0[l], cache_a1[l], cache_a2[l]),
                                 state_ret[l], state_wkv[l], state_shift[l], p, w_exp_gu, w_exp_down, l)
        for lst, v in zip(new_p, st_p):
            lst.append(v)
        for lst, v in zip(new_s, st_s):
            lst.append(v)
    a0_p, a1_p, a2_p, ret_p, wkv_p, shift_p = [jnp.stack(z) for z in new_p]
    a0_s, a1_s, a2_s, ret_s, wkv_s, shift_s = [jnp.stack(z) for z in new_s]
    y_prompt = rms_norm(xp, norm_f)
    y_sample = rms_norm(xs, norm_f)
    return (y_prompt, y_sample, a0_p, a1_p, a2_p, ret_p, wkv_p, shift_p, a0_s, a1_s, a2_s, ret_s, wkv_s, shift_s)
```

```python
import functools
import math

import jax
import jax.numpy as jnp
import numpy as np
from jax import lax
from jax.experimental import pallas as pl
from jax.experimental.pallas import tpu as pltpu

F32 = jnp.float32
BF16 = jnp.bfloat16

D_MODEL = 4096
PAST_LEN = 16384
A_PATTERNS = ((128, 1), (512, 4), (2048, 16))
H_A = 8
HD_A = 128
A_OUT = H_A * HD_A
ROT_A = HD_A // 4
ROPE_THETA = 500000.0
A_COLS = 3 * 3 * A_OUT
H_B = 4
DK_B = 128
DV_B = 256
B_OUT = H_B * DV_B
B_COLS = 2 * H_B * DK_B + 2 * B_OUT
RET_THETA = 10000.0
RET_CHUNK = 128
H_C = 16
HD_C = 64
C_W = H_C * HD_C
LORA_W = 64
LORA_A = 64
LORA_G = 160
C_COLS = 3 * C_W + LORA_W + LORA_A + LORA_G
C_PAD = 3456
LORA_PAD = C_PAD - 3 * C_W
WKV_GN_EPS = 64e-5
N_EXPERTS = 128
N_EXP_GROUPS = 8
TOPK_GROUPS = 4
TOP_K = 8
D_EXPERT = 768
ROUTED_SCALE = 2.5
NORM_EPS = 1e-6
SAMPLE_T_PAD = 8

VMEM_LIMIT = 56 * 1024 * 1024
WKV_L = 32
WKV_G = 4


def _cparams(sem):
    return pltpu.CompilerParams(dimension_semantics=sem, vmem_limit_bytes=VMEM_LIMIT)


def _bdot(a, b):
    return jnp.dot(a.astype(BF16), b.astype(BF16), preferred_element_type=F32)


def _bdot_t(a, b):
    return lax.dot_general(a.astype(BF16), b.astype(BF16), (((1,), (1,)), ((), ())),
                           preferred_element_type=F32)


def _split3(a):
    a1 = a.astype(BF16)
    r1 = a - a1.astype(F32)
    a2 = r1.astype(BF16)
    a3 = (r1 - a2.astype(F32)).astype(BF16)
    return a1, a2, a3


def _dot_general3(a, b, dims):
    a1, a2, a3 = _split3(a)
    b1, b2, b3 = _split3(b)
    dg = functools.partial(lax.dot_general, dimension_numbers=dims, preferred_element_type=F32)
    small = dg(a1, b3) + dg(a3, b1) + dg(a2, b2)
    mid = dg(a1, b2) + dg(a2, b1)
    return dg(a1, b1) + (mid + small)


_NN = (((1,), (0,)), ((), ()))
_NT = (((1,), (1,)), ((), ()))
_TN = (((0,), (0,)), ((), ()))


def _silu(x):
    return x * jax.nn.sigmoid(x)


def _ada_kernel(c_ref, w_ref, b_ref, o_ref):
    o_ref[0] = _bdot(_silu(c_ref[...]), w_ref[0]) + b_ref[0]


def ada_mod(c_all, w_ada, b_ada):
    n_layers, d, n = w_ada.shape
    rows = c_all.shape[0]
    tn = 512
    return pl.pallas_call(
        _ada_kernel,
        grid=(n_layers, n // tn),
        in_specs=[pl.BlockSpec((rows, d), lambda l, j: (0, 0)),
                  pl.BlockSpec((1, d, tn), lambda l, j: (l, 0, j)),
                  pl.BlockSpec((1, 1, tn), lambda l, j: (l, 0, j))],
        out_specs=pl.BlockSpec((1, rows, tn), lambda l, j: (l, 0, j)),
        out_shape=jax.ShapeDtypeStruct((n_layers, rows, n), F32),
        compiler_params=_cparams(("arbitrary", "arbitrary")),
        name="ada_mod",
    )(c_all, w_ada, b_ada.reshape(n_layers, 1, n))


def _row_mod_spec(mod, tile, rows_per_group, tn=None, n_tiles=None):
    per_group = mod.shape[1] == 1
    clamp = (lambda i: i) if n_tiles is None else (lambda i: jnp.minimum(i, n_tiles - 1))
    lead = ((lambda i: (clamp(i) * tile // rows_per_group, 0)) if per_group else (lambda i: (0, clamp(i))))
    rows = 1 if per_group else tile
    if tn is None:
        return pl.BlockSpec((1, rows, mod.shape[2]), lambda i: (*lead(i), 0))
    return pl.BlockSpec((1, rows, tn), lambda i, j: (*lead(i), j))


def _norm_mod_kernel(x_ref, g_ref, sc_ref, sh_ref, *rest, n_tiles):
    o_ref = rest[-1]

    @pl.when(pl.program_id(0) < n_tiles)
    def _rows():
        x = x_ref[...]
        y = x * lax.rsqrt(jnp.mean(x * x, axis=-1, keepdims=True) + NORM_EPS) * g_ref[...]
        o_ref[...] = (y * (1.0 + sc_ref[0]) + sh_ref[0]).astype(o_ref.dtype)

    @pl.when(pl.program_id(0) >= n_tiles)
    def _tail():
        o_ref[...] = jnp.zeros_like(o_ref)


def norm_mod(x2, g, sc, sh, tile, rows_per_group, out_dtype, into=None, into_row0=0, out_rows=None):
    m, d = x2.shape
    n_tiles = m // tile
    spec = _row_mod_spec(sc, tile, rows_per_group, n_tiles=n_tiles)
    in_specs = [pl.BlockSpec((tile, d), lambda i: (jnp.minimum(i, n_tiles - 1), 0)),
                pl.BlockSpec((1, d), lambda i: (0, 0)), spec, spec]
    args = [x2, g.reshape(1, d), sc, sh]
    rb0 = into_row0 // tile
    aliases = {}
    out_shape = jax.ShapeDtypeStruct((m if out_rows is None else out_rows, d), out_dtype)
    if into is not None:
        in_specs.append(pl.BlockSpec(memory_space=pl.ANY))
        args.append(into)
        aliases = {4: 0}
        out_shape = jax.ShapeDtypeStruct(into.shape, into.dtype)
    return pl.pallas_call(
        functools.partial(_norm_mod_kernel, n_tiles=n_tiles),
        grid=(n_tiles if out_rows is None else pl.cdiv(out_rows, tile),),
        in_specs=in_specs,
        out_specs=pl.BlockSpec((tile, d), lambda i: (rb0 + i, 0)),
        out_shape=out_shape,
        input_output_aliases=aliases,
        compiler_params=_cparams(("arbitrary",)),
        name="norm_mod",
    )(*args)


def _final_norm_kernel(x_ref, g_ref, o_ref):
    x = x_ref[...]
    o_ref[...] = x * lax.rsqrt(jnp.mean(x * x, axis=-1, keepdims=True) + NORM_EPS) * g_ref[...]


def final_norm(x2, g, tile):
    m, d = x2.shape
    return pl.pallas_call(
        _final_norm_kernel,
        grid=(m // tile,),
        in_specs=[pl.BlockSpec((tile, d), lambda i: (i, 0)), pl.BlockSpec((1, d), lambda i: (0, 0))],
        out_specs=pl.BlockSpec((tile, d), lambda i: (i, 0)),
        out_shape=jax.ShapeDtypeStruct((m, d), F32),
        compiler_params=_cparams(("arbitrary",)),
        name="final_norm",
    )(x2, g.reshape(1, d))


def _mm_kernel(x_ref, w_ref, o_ref):
    o_ref[...] = _bdot(x_ref[...], w_ref[...]).astype(o_ref.dtype)


def matmul(x, w, tm, tn, out_dtype=F32, row0=0, rows=None):
    k = x.shape[1]
    rows = x.shape[0] - row0 if rows is None else rows
    n = w.shape[1]
    rb0 = row0 // tm
    return pl.pallas_call(
        _mm_kernel,
        grid=(pl.cdiv(rows, tm), n // tn),
        in_specs=[pl.BlockSpec((tm, k), lambda i, j: (rb0 + i, 0)),
                  pl.BlockSpec((k, tn), lambda i, j: (0, j))],
        out_specs=pl.BlockSpec((tm, tn), lambda i, j: (i, j)),
        out_shape=jax.ShapeDtypeStruct((rows, n), out_dtype),
        compiler_params=_cparams(("arbitrary", "arbitrary")),
        name="proj",
    )(x, w)


def rope_tables_a(pos):
    half = ROT_A // 2
    inv = ROPE_THETA ** (-jnp.arange(half, dtype=F32) / half)
    ang = pos.astype(F32)[:, None] * inv[None, :]
    cos, sin = jnp.cos(ang), jnp.sin(ang)
    t = pos.shape[0]
    one = jnp.ones((t, HD_A - ROT_A), F32)
    zero_tail = jnp.zeros((t, HD_A - ROT_A), F32)
    zero_h = jnp.zeros((t, half), F32)
    c = jnp.concatenate([cos, cos, one], axis=1)
    sp = jnp.concatenate([zero_h, sin, zero_tail], axis=1)
    sm = jnp.concatenate([-sin, zero_h, zero_tail], axis=1)
    return c, sp, sm


def rope_tables_b(pos):
    half = DK_B // 2
    inv = RET_THETA ** (-jnp.arange(half, dtype=F32) / half)
    ang = pos.astype(F32)[:, None] * inv[None, :]
    cos, sin = jnp.cos(ang), jnp.sin(ang)
    return jnp.concatenate([cos, cos], axis=1), jnp.concatenate([-sin, sin], axis=1)


def _rope_a(x, c, sp, sm):
    half = ROT_A // 2
    return x * c + pltpu.roll(x, half, 1) * sp + pltpu.roll(x, HD_A - half, 1) * sm


def _attn_prompt_kernel(q0, k0, v0, q1, k1, v1, q2, k2, v2, c_ref, sp_ref, sm_ref, o_ref,
                        qr, kr, og, mg, sg):
    t = qr.shape[0]
    blk = 128
    n_iter = t // blk
    qi = lax.broadcasted_iota(jnp.int32, (blk, blk), 0)
    ki = lax.broadcasted_iota(jnp.int32, (blk, blk), 1)
    cur_ok = qi >= ki
    prev_ok = ki >= qi
    neg = jnp.float32(-jnp.inf)
    scale = HD_A ** -0.5
    for g, (qf, kf, vf) in enumerate(((q0, k0, v0), (q1, k1, v1), (q2, k2, v2))):
        dil = A_PATTERNS[g][1]
        nb = t // (dil * blk)
        c, sp, sm = c_ref[...], sp_ref[...], sm_ref[...]
        qr[...] = _rope_a(qf[0], c, sp, sm) * scale
        kr[...] = _rope_a(kf[0], c, sp, sm)

        def body(it, carry, g=g, dil=dil, nb=nb, vf=vf):
            r = it % dil
            n = it // dil
            cur = pl.ds(n * (blk * dil) + r, blk, stride=dil)
            q = qr[cur, :]
            lc = jnp.where(cur_ok, _bdot_t(q, kr[cur, :]), neg)
            m = jnp.max(lc, axis=-1, keepdims=True)
            if nb > 1:
                prev = pl.ds(jnp.maximum(n - 1, 0) * (blk * dil) + r, blk, stride=dil)
                lp = jnp.where(prev_ok & (n > 0), _bdot_t(q, kr[prev, :]), neg)
                m = jnp.maximum(m, jnp.max(lp, axis=-1, keepdims=True))
            pc = jnp.exp(lc - m)
            s = jnp.sum(pc, axis=-1, keepdims=True)
            o = _bdot(pc, vf[0, cur, :])
            if nb > 1:
                pp = jnp.exp(lp - m)
                s = s + jnp.sum(pp, axis=-1, keepdims=True)
                o = o + _bdot(pp, vf[0, prev, :])
            og[g, cur, :] = o
            mg[g, cur, :] = jnp.broadcast_to(m, (blk, HD_A))
            sg[g, cur, :] = jnp.broadcast_to(s, (blk, HD_A))
            return carry

        lax.fori_loop(0, n_iter, body, 0)
    m_all = jnp.maximum(jnp.maximum(mg[0], mg[1]), mg[2])
    num = jnp.zeros_like(m_all)
    den = jnp.zeros_like(m_all)
    for g in range(3):
        e = jnp.exp(mg[g] - m_all)
        num = num + e * og[g]
        den = den + e * sg[g]
    o_ref[0] = (num / den).astype(o_ref.dtype)


def attn_prompt(za, tabs):
    b, t, _ = za.shape
    assert t % (A_PATTERNS[-1][1] * 128) == 0
    col_specs = []
    for g in range(3):
        for j in range(3):
            col_specs.append(pl.BlockSpec((1, t, HD_A), lambda bi, h, cb=(g * 3 + j) * H_A: (bi, 0, cb + h)))
    tab_spec = pl.BlockSpec((t, HD_A), lambda bi, h: (0, 0))
    return pl.pallas_call(
        _attn_prompt_kernel,
        grid=(b, H_A),
        in_specs=col_specs + [tab_spec] * 3,
        out_specs=pl.BlockSpec((1, t, HD_A), lambda bi, h: (bi, 0, h)),
        out_shape=jax.ShapeDtypeStruct((b, t, A_OUT), BF16),
        scratch_shapes=[pltpu.VMEM((t, HD_A), F32), pltpu.VMEM((t, HD_A), F32),
                        pltpu.VMEM((3, t, HD_A), F32), pltpu.VMEM((3, t, HD_A), F32),
                        pltpu.VMEM((3, t, HD_A), F32)],
        compiler_params=_cparams(("arbitrary", "arbitrary")),
        name="attn_prompt",
    )(*([za] * 9), *tabs)


def _kv_rows_kernel(k_ref, v_ref, c_ref, sp_ref, sm_ref, o_ref):
    rt = k_ref.shape[1]
    c, sp, sm = c_ref[...], sp_ref[...], sm_ref[...]
    for h in range(H_A):
        cols = slice(h * HD_A, (h + 1) * HD_A)
        o_ref[0, pl.ds(h, rt, stride=2 * H_A), :] = _rope_a(k_ref[0, :, cols], c, sp, sm)
        o_ref[0, pl.ds(H_A + h, rt, stride=2 * H_A), :] = v_ref[0, :, cols]


def kv_rows(za, tabs, g, rows, rt):
    b, t, _ = za.shape
    off = (t - rows) // rt
    tab_spec = pl.BlockSpec((rt, HD_A), lambda bi, i: (off + i, 0))
    return pl.pallas_call(
        _kv_rows_kernel,
        grid=(b, rows // rt),
        in_specs=[pl.BlockSpec((1, rt, A_OUT), lambda bi, i: (bi, off + i, g * 3 + 1)),
                  pl.BlockSpec((1, rt, A_OUT), lambda bi, i: (bi, off + i, g * 3 + 2)),
                  tab_spec, tab_spec, tab_spec],
        out_specs=pl.BlockSpec((1, rt * 2 * H_A, HD_A), lambda bi, i: (bi, i, 0)),
        out_shape=jax.ShapeDtypeStruct((b, rows * 2 * H_A, HD_A), F32),
        compiler_params=_cparams(("arbitrary", "arbitrary")),
        name="kv_rows",
    )(za, za, *tabs)


def _attn_sample_kernel(q_ref, k_ref, v_ref, cache_ref, c_ref, sp_ref, sm_ref, num_ref, den_ref, m_ref,
                        onum_ref, oden_ref, om_ref, *, dil, n_back, final):
    rows = q_ref.shape[1]
    lb = cache_ref.shape[2]
    c, sp, sm = c_ref[...], sp_ref[...], sm_ref[...]
    q = _rope_a(q_ref[0], c, sp, sm) * (HD_A ** -0.5)
    kn = _rope_a(k_ref[0], c, sp, sm)
    kc = cache_ref[0, 0, :, 0, :, :].reshape(lb * H_A, HD_A)
    vc = cache_ref[0, 0, :, 1, :, :].reshape(lb * H_A, HD_A)
    neg = jnp.float32(-jnp.inf)
    span = n_back * dil

    row = lax.broadcasted_iota(jnp.int32, (rows, lb * H_A), 0)
    col = lax.broadcasted_iota(jnp.int32, (rows, lb * H_A), 1)
    dist = lb + (row // H_A) - (col // H_A)
    ok = ((col % H_A) == (row % H_A)) & ((dist % dil) == 0) & (dist <= span)
    lc = jnp.where(ok, _bdot_t(q, kc), neg)

    rown = lax.broadcasted_iota(jnp.int32, (rows, rows), 0)
    coln = lax.broadcasted_iota(jnp.int32, (rows, rows), 1)
    dn = (rown // H_A) - (coln // H_A)
    okn = ((coln % H_A) == (rown % H_A)) & (dn >= 0) & ((dn % dil) == 0) & (dn <= span)
    ln = jnp.where(okn, _bdot_t(q, kn), neg)

    m = jnp.maximum(jnp.max(lc, axis=-1, keepdims=True), jnp.max(ln, axis=-1, keepdims=True))
    pc = jnp.exp(lc - m)
    pn = jnp.exp(ln - m)
    s = jnp.sum(pc, axis=-1, keepdims=True) + jnp.sum(pn, axis=-1, keepdims=True)
    o = _bdot(pc, vc) + _bdot(pn, v_ref[0])

    m_prev = m_ref[0]
    m_new = jnp.maximum(m_prev, m)
    e_prev = jnp.exp(m_prev - m_new)
    e_cur = jnp.exp(m - m_new)
    num = num_ref[0] * e_prev + o * e_cur
    den = den_ref[0] * e_prev + s * e_cur
    onum_ref[0] = num / den if final else num
    oden_ref[0] = den
    om_ref[0] = m_new


def attn_sample(qkv, cache, layer, tabs, state, g, final):
    b, rows, _ = qkv[0].shape
    lb = cache.shape[2]
    win, dil = A_PATTERNS[g]
    row_spec = pl.BlockSpec((1, rows, HD_A), lambda bi: (bi, 0, 0))
    tab_spec = pl.BlockSpec((rows, HD_A), lambda bi: (0, 0))
    st = jax.ShapeDtypeStruct((b, rows, HD_A), F32)
    return pl.pallas_call(
        functools.partial(_attn_sample_kernel, dil=dil, n_back=win // dil, final=final),
        grid=(b,),
        in_specs=[row_spec, row_spec, row_spec,
                  pl.BlockSpec((1, 1, lb, 2, H_A, HD_A), lambda bi: (layer, bi, 0, 0, 0, 0)),
                  tab_spec, tab_spec, tab_spec, row_spec, row_spec, row_spec],
        out_specs=[row_spec, row_spec, row_spec],
        out_shape=[st, st, st],
        compiler_params=_cparams(("arbitrary",)),
        name="attn_sample",
    )(*qkv, cache, *tabs, *state)


def retention_tables(lc, l_real):
    log_gamma = np.log(1.0 - 2.0 ** (-5.0 - np.arange(H_B, dtype=np.float32))).astype(np.float32)
    i = np.arange(lc, dtype=np.float32)
    diff = i[:, None] - i[None, :]
    real = (i < l_real)
    inner = np.where((diff >= 0) & real[None, :], np.exp(np.maximum(diff, 0.0)[None] * log_gamma[:, None, None]), 0.0)
    qdec = np.exp((i + 1.0)[None, :] * log_gamma[:, None])[:, :, None]
    kdec = np.where(real[None, :], np.exp((l_real - 1.0 - i)[None, :] * log_gamma[:, None]), 0.0)[:, :, None]
    gl = np.exp(l_real * log_gamma)[:, None, None]
    f = lambda z: jnp.asarray(z, F32)
    return f(inner), f(qdec), f(kdec), f(gl)


def _retention_kernel(q_ref, k_ref, v_ref, g_ref, c_ref, s_ref, dm_ref, qd_ref, kd_ref, gl_ref, s0_ref,
                      o_ref, so_ref, *, lc):
    t = q_ref.shape[1]
    so_ref[0, 0] = s0_ref[0, 0]
    dm = dm_ref[0]
    qd = qd_ref[0]
    kd = kd_ref[0]
    gl = gl_ref[0]

    def body(ci, carry):
        rows = pl.ds(pl.multiple_of(ci * lc, lc), lc)
        c, s = c_ref[rows, :], s_ref[rows, :]
        qx, kx = q_ref[0, rows, :], k_ref[0, rows, :]
        q = qx * c + pltpu.roll(qx, DK_B // 2, 1) * s
        k = (kx * c + pltpu.roll(kx, DK_B // 2, 1) * s) * (DK_B ** -0.5)
        v = v_ref[0, rows, :]
        st = so_ref[0, 0]
        o = _bdot(_bdot_t(q, k) * dm, v) + _bdot(q, st) * qd
        so_ref[0, 0] = st * gl + lax.dot_general((k * kd).astype(BF16), v.astype(BF16), _TN,
                                                 preferred_element_type=F32)
        o = o * lax.rsqrt(jnp.mean(o * o, axis=-1, keepdims=True) + NORM_EPS)
        o_ref[0, rows, :] = (_silu(g_ref[0, rows, :]) * o).astype(o_ref.dtype)
        return carry

    lax.fori_loop(0, t // lc, body, 0)


def retention(zb, tabs, dec, s0, lc):
    b, t, _ = zb.shape
    nqk = H_B * DK_B // DV_B
    tab_spec = pl.BlockSpec((t, DK_B), lambda bi, h: (0, 0))
    return pl.pallas_call(
        functools.partial(_retention_kernel, lc=lc),
        grid=(b, H_B),
        in_specs=[pl.BlockSpec((1, t, DK_B), lambda bi, h: (bi, 0, h)),
                  pl.BlockSpec((1, t, DK_B), lambda bi, h: (bi, 0, H_B + h)),
                  pl.BlockSpec((1, t, DV_B), lambda bi, h: (bi, 0, 2 * nqk + h)),
                  pl.BlockSpec((1, t, DV_B), lambda bi, h: (bi, 0, 2 * nqk + H_B + h)),
                  tab_spec, tab_spec,
                  pl.BlockSpec((1, lc, lc), lambda bi, h: (h, 0, 0)),
                  pl.BlockSpec((1, lc, 1), lambda bi, h: (h, 0, 0)),
                  pl.BlockSpec((1, lc, 1), lambda bi, h: (h, 0, 0)),
                  pl.BlockSpec((1, 1, 1), lambda bi, h: (h, 0, 0)),
                  pl.BlockSpec((1, 1, DK_B, DV_B), lambda bi, h: (bi, h, 0, 0))],
        out_specs=[pl.BlockSpec((1, t, DV_B), lambda bi, h: (bi, 0, h)),
                   pl.BlockSpec((1, 1, DK_B, DV_B), lambda bi, h: (bi, h, 0, 0))],
        out_shape=[jax.ShapeDtypeStruct((b, t, B_OUT), BF16),
                   jax.ShapeDtypeStruct((b, H_B, DK_B, DV_B), F32)],
        compiler_params=_cparams(("arbitrary", "arbitrary")),
        name="retention",
    )(zb, zb, zb, zb, *tabs, *dec, s0)


def _seg_sum(x, seg_ref):
    x1 = x.astype(BF16)
    x2 = (x - x1.astype(F32)).astype(BF16)
    seg = seg_ref[...]
    return jnp.dot(x1, seg, preferred_element_type=F32) + jnp.dot(x2, seg, preferred_element_type=F32)


def _wkv_prep_kernel(zc_ref, zp_ref, sh_ref, mu_ref, wl_ref, w0_ref, a0_ref, kk_ref, ka_ref, rk_ref,
                     seg_ref, tri_ref, ones_ref,
                     rt_ref, at_ref, bt_ref, kt_ref, bh_ref, kh_ref, v_ref, bon_ref, g_ref, wlast_ref,
                     *, t_valid):
    i = pl.program_id(1)
    tt = zc_ref.shape[1]
    zc = zc_ref[0]
    row = lax.broadcasted_iota(jnp.int32, (tt, 1), 0)
    first = jnp.where(i == 0, sh_ref[0], zp_ref[0, 7:8, :])
    prev = jnp.where(row == 0, first, pltpu.roll(zc, 1, 0))
    xc = zc + (prev - zc) * mu_ref[...]
    r = xc[:, 0:C_W]
    k = xc[:, C_W:2 * C_W]
    v = xc[:, 2 * C_W:3 * C_W]
    xl = xc[:, 3 * C_W:]
    lane = lax.broadcasted_iota(jnp.int32, xl.shape, 1)
    act = jnp.where(lane < LORA_W, jnp.tanh(xl),
                    jnp.where(lane < LORA_W + LORA_A, xl,
                              jnp.where(lane < LORA_W + LORA_A + LORA_G, jax.nn.sigmoid(xl), 0.0)))
    lora = _bdot(act, wl_ref[...])
    wl = w0_ref[...] + lora[:, 0:C_W]
    softplus = jnp.maximum(-wl, 0.0) + jnp.log(1.0 + jnp.exp(-jnp.abs(wl)))
    lw = -jnp.exp(-softplus - 0.5)
    a = jax.nn.sigmoid(a0_ref[...] + lora[:, C_W:2 * C_W])
    g_ref[0] = lora[:, 2 * C_W:3 * C_W]
    kk = k * kk_ref[...]
    kk = kk / jnp.maximum(jnp.sqrt(_seg_sum(kk * kk, seg_ref)), 1e-12)
    k2 = k * (1.0 + (a - 1.0) * ka_ref[...])
    bon_ref[0] = _seg_sum(r * k2 * rk_ref[...], seg_ref) * v
    al = -kk
    be = kk * a
    if t_valid is not None:
        ok = (i * tt + row) < t_valid
        lw = jnp.where(ok, lw, 0.0)
        al = jnp.where(ok, al, 0.0)
        be = jnp.where(ok, be, 0.0)
        k2 = jnp.where(ok, k2, 0.0)
    l1, l2, l3 = _split3(lw)
    tri = tri_ref[...]
    ones = ones_ref[...]
    dot = functools.partial(jnp.dot, preferred_element_type=F32)
    cum = dot(tri, l1) + (dot(tri, l2) + dot(tri, l3))
    tot = dot(ones, l1) + (dot(ones, l2) + dot(ones, l3))
    winv = jnp.exp(-cum)
    wrest = jnp.exp(tot - cum)
    rt_ref[0] = r * jnp.exp(cum)
    at_ref[0] = al * jnp.exp(cum - lw)
    bt_ref[0] = be * winv
    kt_ref[0] = k2 * winv
    bh_ref[0] = be * wrest
    kh_ref[0] = k2 * wrest
    v_ref[0] = v
    wl_all = jnp.exp(tot)
    for c in range(tt // WKV_L):
        wlast_ref[0, c:c + 1, :] = wl_all[c * WKV_L:c * WKV_L + 1, :]


def _blockdiag_ones(n, blk, lower):
    i = np.arange(n)
    same = (i[:, None] // blk) == (i[None, :] // blk)
    m = same & (i[:, None] >= i[None, :]) if lower else same
    return jnp.asarray(m, BF16)


def wkv_prep(zc, shift_row, p, tt, t_valid):
    b, t, _ = zc.shape
    row = lambda z: z.reshape(1, -1)
    mu = jnp.pad(p['mu_c'], (0, C_PAD - C_COLS)).reshape(1, C_PAD)
    w_lora = jnp.zeros((LORA_PAD, 3 * C_W), F32)
    w_lora = w_lora.at[0:LORA_W, 0:C_W].set(p['w_decay_up'])
    w_lora = w_lora.at[LORA_W:LORA_W + LORA_A, C_W:2 * C_W].set(p['a_up'])
    w_lora = w_lora.at[LORA_W + LORA_A:LORA_W + LORA_A + LORA_G, 2 * C_W:].set(p['g_up']).astype(BF16)
    seg = _blockdiag_ones(C_W, HD_C, False)
    tri = _blockdiag_ones(tt, WKV_L, True)
    ones = _blockdiag_ones(tt, WKV_L, False)
    vec = pl.BlockSpec((1, C_W), lambda bi, i: (0, 0))
    const = lambda shape: pl.BlockSpec(shape, lambda bi, i: (0, 0))
    blk = pl.BlockSpec((1, tt, C_W), lambda bi, i: (bi, i, 0))
    full = jax.ShapeDtypeStruct((b, t, C_W), F32)
    return pl.pallas_call(
        functools.partial(_wkv_prep_kernel, t_valid=t_valid),
        grid=(b, t // tt),
        in_specs=[pl.BlockSpec((1, tt, C_PAD), lambda bi, i: (bi, i, 0)),
                  pl.BlockSpec((1, 8, C_PAD), lambda bi, i: (bi, jnp.maximum(i * (tt // 8) - 1, 0), 0)),
                  pl.BlockSpec((1, 1, C_PAD), lambda bi, i: (bi, 0, 0)),
                  const((1, C_PAD)), const((LORA_PAD, 3 * C_W)), vec, vec, vec, vec, vec,
                  const((C_W, C_W)), const((tt, tt)), const((tt, tt))],
        out_specs=[blk] * 9 + [pl.BlockSpec((1, tt // WKV_L, C_W), lambda bi, i: (bi, i, 0))],
        out_shape=[full] * 9 + [jax.ShapeDtypeStruct((b, t // WKV_L, C_W), F32)],
        compiler_params=_cparams(("arbitrary", "arbitrary")),
        name="wkv_prep",
    )(zc, zc, shift_row, mu, w_lora, row(p['w_decay0']), row(p['a0']), row(p['k_k']), row(p['k_a']),
      row(p['r_k']), seg, tri, ones)


def _wkv_chunk_kernel(rt_ref, at_ref, bt_ref, kt_ref, bh_ref, kh_ref, v_ref, wl_ref, s0_ref,
                      mst_ref, pm_ref, eye_ref, y_ref, so_ref):
    t = rt_ref.shape[1]
    n = WKV_G * WKV_L
    so_ref[0, 0] = s0_ref[0, 0]
    mst = mst_ref[...]
    pm = pm_ref[...]
    eye = eye_ref[...]

    def stack(x):
        return (jnp.concatenate([x] * WKV_G, axis=0) * mst).astype(BF16)

    def body(ci, carry):
        rows = pl.ds(pl.multiple_of(ci * WKV_L, WKV_L), WKV_L)
        ld = lambda ref: stack(ref[0, rows, :])
        lhs = jnp.concatenate([ld(at_ref), ld(rt_ref)], axis=0)
        rhs = jnp.concatenate([ld(bt_ref), ld(kt_ref)], axis=0)
        vst = ld(v_ref)
        s_bd = so_ref[0, 0]
        p = _bdot_t(lhs, rhs) * pm
        q = _bdot_t(lhs, s_bd)
        nil = p[:n, :n]
        x = eye + nil
        for _ in range(int(math.log2(WKV_L)) - 1):
            nil = _bdot(nil, nil)
            x = x + _bdot(x, nil)
        u = _bdot(x, q[:n] + _bdot(p[:n, n:], vst))
        uv = jnp.concatenate([u.astype(BF16), vst], axis=0)
        y = q[n:] + _bdot(p[n:], uv)
        acc = y[0:WKV_L]
        for g in range(1, WKV_G):
            acc = acc + y[g * WKV_L:(g + 1) * WKV_L]
        y_ref[0, rows, :] = acc
        hat = jnp.concatenate([ld(bh_ref), ld(kh_ref)], axis=0)
        so_ref[0, 0] = s_bd * wl_ref[0, pl.ds(ci, 1), :] + lax.dot_general(
            uv, hat, _TN, preferred_element_type=F32)
        return carry

    lax.fori_loop(0, t // WKV_L, body, 0)


def wkv_chunk(prep, s0):
    rt, at, bt, kt, bh, kh, v, _, _, wlast = prep
    b, t, _ = rt.shape
    ng = H_C // WKV_G
    w = WKV_G * HD_C
    n = WKV_G * WKV_L
    s0g = s0.astype(F32).reshape(b, ng, WKV_G, HD_C, HD_C)
    eye_g = jnp.eye(WKV_G, dtype=F32)
    s_bd = jnp.einsum('bngij,gh->bngihj', s0g, eye_g).reshape(b, ng, w, w)
    mst = jnp.asarray((np.arange(n)[:, None] // WKV_L) == (np.arange(w)[None, :] // HD_C), F32)
    pi = np.arange(2 * n)
    pm_np = np.where((pi[:, None] < n), (pi[:, None] % n) > (pi[None, :] % n), (pi[:, None] % n) >= (pi[None, :] % n))
    pm = jnp.asarray(pm_np, F32)
    eye = jnp.eye(n, dtype=F32)
    blk = pl.BlockSpec((1, t, w), lambda bi, h: (bi, 0, h))
    const = lambda shape: pl.BlockSpec(shape, lambda bi, h: (0, 0))
    st_spec = pl.BlockSpec((1, 1, w, w), lambda bi, h: (bi, h, 0, 0))
    y, s_new = pl.pallas_call(
        _wkv_chunk_kernel,
        grid=(b, ng),
        in_specs=[blk] * 7 + [pl.BlockSpec((1, t // WKV_L, w), lambda bi, h: (bi, 0, h)), st_spec,
                              const((n, w)), const((2 * n, 2 * n)), const((n, n))],
        out_specs=[blk, st_spec],
        out_shape=[jax.ShapeDtypeStruct((b, t, C_W), F32), jax.ShapeDtypeStruct((b, ng, w, w), F32)],
        compiler_params=_cparams(("arbitrary", "arbitrary")),
        name="wkv_chunk",
    )(rt, at, bt, kt, bh, kh, v, wlast, s_bd, mst, pm, eye)
    s_new = jnp.einsum('bngihj,gh->bngij', s_new.reshape(b, ng, WKV_G, HD_C, WKV_G, HD_C), eye_g)
    return y, s_new.reshape(b, H_C, HD_C, HD_C)


def _wkv_post_kernel(y_ref, bon_ref, g_ref, lw_ref, lb_ref, seg_ref, o_ref):
    y = y_ref[0]
    mean = _seg_sum(y, seg_ref) * (1.0 / HD_C)
    d = y - mean
    var = _seg_sum(d * d, seg_ref) * (1.0 / HD_C)
    yn = d * lax.rsqrt(var + WKV_GN_EPS) * lw_ref[...] + lb_ref[...]
    o_ref[0] = ((yn + bon_ref[0]) * g_ref[0]).astype(o_ref.dtype)


def wkv_post(y, bon, g, lnx_w, lnx_b, tt):
    b, t, _ = y.shape
    blk = pl.BlockSpec((1, tt, C_W), lambda bi, i: (bi, i, 0))
    vec = pl.BlockSpec((1, C_W), lambda bi, i: (0, 0))
    return pl.pallas_call(
        _wkv_post_kernel,
        grid=(b, t // tt),
        in_specs=[blk, blk, blk, vec, vec, pl.BlockSpec((C_W, C_W), lambda bi, i: (0, 0))],
        out_specs=blk,
        out_shape=jax.ShapeDtypeStruct((b, t, C_W), BF16),
        compiler_params=_cparams(("arbitrary", "arbitrary")),
        name="wkv_post",
    )(y, bon, g, lnx_w.reshape(1, C_W), lnx_b.reshape(1, C_W), _blockdiag_ones(C_W, HD_C, False))


def _upmerge_kernel(oa_ref, ob_ref, oc_ref, wa_ref, wb_ref, wc_ref, g0_ref, g1_ref, g2_ref, o_ref):
    acc = jax.nn.sigmoid(g0_ref[...]) * jnp.dot(oa_ref[...], wa_ref[...], preferred_element_type=F32)
    acc = acc + jax.nn.sigmoid(g1_ref[...]) * jnp.dot(ob_ref[...], wb_ref[...], preferred_element_type=F32)
    acc = acc + jax.nn.sigmoid(g2_ref[...]) * jnp.dot(oc_ref[...], wc_ref[...], preferred_element_type=F32)
    o_ref[...] = acc.astype(o_ref.dtype)


def upmerge(oa, ob, oc, wa, wb, wc, zg, tm, tn):
    m = oa.shape[0]
    d = wa.shape[1]
    nj = d // tn
    act = pl.BlockSpec((tm, oa.shape[1]), lambda i, j: (i, 0))
    wsp = pl.BlockSpec((wa.shape[0], tn), lambda i, j: (0, j))
    gate = lambda br: pl.BlockSpec((tm, tn), lambda i, j: (i, br * nj + j))
    return pl.pallas_call(
        _upmerge_kernel,
        grid=(m // tm, nj),
        in_specs=[act, act, act, wsp, wsp, wsp, gate(0), gate(1), gate(2)],
        out_specs=pl.BlockSpec((tm, tn), lambda i, j: (i, j)),
        out_shape=jax.ShapeDtypeStruct((m, d), BF16),
        compiler_params=_cparams(("arbitrary", "arbitrary")),
        name="upmerge",
    )(oa, ob, oc, wa, wb, wc, zg, zg, zg)


def _proj_res_kernel(a_ref, w_ref, x_ref, gt_ref, o_ref):
    o_ref[...] = x_ref[...] + gt_ref[0] * _bdot(a_ref[...], w_ref[...])


def proj_residual(a, w, x, gt, tm, tn, rows_per_group):
    m, k = a.shape
    n = w.shape[1]
    return pl.pallas_call(
        _proj_res_kernel,
        grid=(m // tm, n // tn),
        in_specs=[pl.BlockSpec((tm, k), lambda i, j: (i, 0)),
                  pl.BlockSpec((k, tn), lambda i, j: (0, j)),
                  pl.BlockSpec((tm, tn), lambda i, j: (i, j)),
                  _row_mod_spec(gt, tm, rows_per_group, tn)],
        out_specs=pl.BlockSpec((tm, tn), lambda i, j: (i, j)),
        out_shape=jax.ShapeDtypeStruct((m, n), F32),
        compiler_params=_cparams(("arbitrary", "arbitrary")),
        name="proj_residual",
    )(a, w, x, gt)


EXP_SUB = 128
EXP_NSUB = 5
EXP_BMAX = EXP_SUB * EXP_NSUB
EXP_CE = 256
EXP_NC = D_EXPERT // EXP_CE
COMB_T = 32


def _router_kernel(h_ref, w_ref, b_ref, idx_ref, wts_ref):
    h1, h2, _ = _split3(h_ref[...])
    w1, w2, _ = _split3(w_ref[...])
    dot = functools.partial(jnp.dot, preferred_element_type=F32)
    logits = dot(h1, w1) + (dot(h1, w2) + dot(h2, w1))
    scores = jax.nn.sigmoid(logits)
    choice = scores + b_ref[...]
    tm = choice.shape[0]
    lane = lax.broadcasted_iota(jnp.int32, (tm, N_EXPERTS), 1)
    per_group = N_EXPERTS // N_EXP_GROUPS
    grp = lane // per_group
    neg = jnp.float32(-jnp.inf)

    def first_argmax(x, ids, sentinel):
        m = jnp.max(x, axis=-1, keepdims=True)
        return m, jnp.min(jnp.where(x == m, ids, sentinel), axis=-1, keepdims=True)

    gscore = jnp.zeros_like(choice)
    for g in range(N_EXP_GROUPS):
        xg = jnp.where(grp == g, choice, neg)
        m1, i1 = first_argmax(xg, lane, N_EXPERTS)
        m2 = jnp.max(jnp.where(lane == i1, neg, xg), axis=-1, keepdims=True)
        gscore = jnp.where(grp == g, m1 + m2, gscore)
    keep = jnp.zeros(choice.shape, jnp.bool_)
    for _ in range(TOPK_GROUPS):
        _, gi = first_argmax(gscore, grp, N_EXP_GROUPS)
        keep = keep | (grp == gi)
        gscore = jnp.where(grp == gi, neg, gscore)
    masked = jnp.where(keep, choice, neg)
    lane_k = lax.broadcasted_iota(jnp.int32, (tm, TOP_K), 1)
    idx = jnp.zeros((tm, TOP_K), jnp.int32)
    wts = jnp.zeros((tm, TOP_K), F32)
    for k in range(TOP_K):
        _, ik = first_argmax(masked, lane, N_EXPERTS)
        hit = lane == ik
        wk = jnp.sum(jnp.where(hit, scores, 0.0), axis=-1, keepdims=True)
        masked = jnp.where(hit, neg, masked)
        idx = jnp.where(lane_k == k, ik, idx)
        wts = jnp.where(lane_k == k, wk, wts)
    wts_ref[...] = wts / jnp.sum(wts, axis=-1, keepdims=True) * ROUTED_SCALE
    idx_ref[...] = idx


def router(h, w_router, b_router, tm):
    n, d = h.shape
    return pl.pallas_call(
        _router_kernel,
        grid=(pl.cdiv(n, tm),),
        in_specs=[pl.BlockSpec((tm, d), lambda i: (i, 0)),
                  pl.BlockSpec((d, N_EXPERTS), lambda i: (0, 0)),
                  pl.BlockSpec((1, N_EXPERTS), lambda i: (0, 0))],
        out_specs=[pl.BlockSpec((tm, TOP_K), lambda i: (i, 0)), pl.BlockSpec((tm, TOP_K), lambda i: (i, 0))],
        out_shape=[jax.ShapeDtypeStruct((n, TOP_K), jnp.int32), jax.ShapeDtypeStruct((n, TOP_K), F32)],
        compiler_params=_cparams(("arbitrary",)),
        name="router",
    )(h, w_router, b_router.reshape(1, N_EXPERTS))


def route_plan(idx, n_blocks):
    n = idx.shape[0]
    n_assign = n * TOP_K
    flat_e = idx.reshape(n_assign)
    order = jnp.argsort(flat_e)
    sorted_e = flat_e[order]
    counts = jnp.sum((flat_e[:, None] == jnp.arange(N_EXPERTS, dtype=jnp.int32)[None, :]).astype(jnp.int32), axis=0)
    padded = (counts + EXP_SUB - 1) // EXP_SUB * EXP_SUB
    pad_end = jnp.cumsum(padded)
    pad_start = pad_end - padded
    cstart = jnp.cumsum(counts) - counts
    slot_sorted = pad_start[sorted_e] + jnp.arange(n_assign, dtype=jnp.int32) - cstart[sorted_e]
    n_slots = n_assign + N_EXPERTS * EXP_SUB
    tok_slots = jnp.zeros((n_slots,), jnp.int32).at[slot_sorted].set((order // TOP_K).astype(jnp.int32))
    slot_of = jnp.zeros((n_assign,), jnp.int32).at[order].set(slot_sorted.astype(jnp.int32)).reshape(n, TOP_K)
    nblk = (counts + EXP_BMAX - 1) // EXP_BMAX
    blk_end = jnp.cumsum(nblk)
    bid = jnp.arange(n_blocks, dtype=jnp.int32)
    used = bid < blk_end[-1]
    be = jnp.minimum(jnp.searchsorted(blk_end, bid, side='right'), N_EXPERTS - 1).astype(jnp.int32)
    within = bid - (blk_end - nblk)[be]
    bcnt = jnp.where(used, jnp.clip(counts[be] - within * EXP_BMAX, 0, EXP_BMAX), 0).astype(jnp.int32)
    bstart = jnp.where(used, pad_start[be] + within * EXP_BMAX,
                       pad_end[-1] + (bid - blk_end[-1]) * EXP_BMAX).astype(jnp.int32)
    last_e = be[jnp.maximum(blk_end[-1] - 1, 0)]
    be = jnp.where(used, be, last_e).astype(jnp.int32)
    return tok_slots.reshape(n_slots // EXP_SUB, EXP_SUB), slot_of, be, bstart // EXP_SUB, bcnt


def _expert_kernel(be_ref, bsub_ref, bcnt_ref, tok_hbm, h_hbm, wg_ref, wu_ref, wd_ref, ys_hbm,
                   tok_smem, stage, xb, wgu_b, wd_b, acc, sem_tok, sem_in, sem_out):
    vb = pl.program_id(0)
    c = pl.program_id(1)
    cnt = bcnt_ref[vb]
    sub0 = bsub_ref[vb]
    nsub = (cnt + EXP_SUB - 1) // EXP_SUB

    @pl.when((vb == 0) & (c == 0))
    def _init():
        xb[...] = jnp.zeros_like(xb)

    @pl.when((cnt == 0) & (c == 0))
    def _zero_tail():
        n_sub_total = ys_hbm.shape[0] // EXP_SUB
        stage[...] = jnp.zeros_like(stage)

        def zero_copy(s):
            return pltpu.make_async_copy(stage, ys_hbm.at[pl.ds((sub0 + s) * EXP_SUB, EXP_SUB)], sem_out)
        for s in range(EXP_NSUB):
            @pl.when(sub0 + s < n_sub_total)
            def _start(s=s):
                zero_copy(s).start()
        for s in range(EXP_NSUB):
            @pl.when(sub0 + s < n_sub_total)
            def _wait(s=s):
                zero_copy(s).wait()

    @pl.when(cnt > 0)
    def _block():
        @pl.when(c == 0)
        def _gather():
            for s in range(EXP_NSUB):
                @pl.when(s < nsub)
                def _sub(s=s):
                    tcp = pltpu.make_async_copy(tok_hbm.at[pl.ds(sub0 + s, 1)], tok_smem, sem_tok)
                    tcp.start()
                    tcp.wait()

                    def row_copy(r):
                        return pltpu.make_async_copy(h_hbm.at[pl.ds(tok_smem[0, r], 1)], stage.at[pl.ds(r, 1)], sem_in)

                    def issue(r, carry):
                        row_copy(r).start()
                        return carry

                    def drain(r, carry):
                        row_copy(r).wait()
                        return carry

                    lax.fori_loop(0, EXP_SUB, issue, 0)
                    lax.fori_loop(0, EXP_SUB, drain, 0)
                    xb[s * EXP_SUB:(s + 1) * EXP_SUB, :] = stage[...].astype(BF16)

        wgu_b[:, :EXP_CE] = wg_ref[0, 0].astype(BF16)
        wgu_b[:, EXP_CE:] = wu_ref[0, 0].astype(BF16)
        wd_b[...] = wd_ref[0, 0].astype(BF16)
        hid = jnp.dot(xb[...], wgu_b[...], preferred_element_type=F32)
        y = _bdot(_silu(hid[:, :EXP_CE]) * hid[:, EXP_CE:], wd_b[...])

        @pl.when(c == 0)
        def _first():
            acc[...] = y

        @pl.when(c > 0)
        def _rest():
            acc[...] += y

        @pl.when(c == EXP_NC - 1)
        def _write():
            def out_copy(s):
                return pltpu.make_async_copy(acc.at[pl.ds(s * EXP_SUB, EXP_SUB)],
                                             ys_hbm.at[pl.ds((sub0 + s) * EXP_SUB, EXP_SUB)], sem_out)
            for s in range(EXP_NSUB):
                @pl.when(s < nsub)
                def _start(s=s):
                    out_copy(s).start()
            for s in range(EXP_NSUB):
                @pl.when(s < nsub)
                def _wait(s=s):
                    out_copy(s).wait()


def routed_experts(h, tok_slots, be, bsub, bcnt, w_gu, w_down, layer):
    n, d = h.shape
    n_slots = tok_slots.shape[0] * EXP_SUB
    n_blocks = be.shape[0]

    def chunk(vb, c, be_ref, bsub_ref, bcnt_ref):
        return jnp.where(bcnt_ref[vb] > 0, c, EXP_NC - 1)

    grid_spec = pltpu.PrefetchScalarGridSpec(
        num_scalar_prefetch=3,
        grid=(n_blocks, EXP_NC),
        in_specs=[pl.BlockSpec(memory_space=pl.ANY),
                  pl.BlockSpec(memory_space=pl.ANY),
                  pl.BlockSpec((1, 1, d, EXP_CE), lambda vb, c, be_r, bs_r, bc_r: (layer, be_r[vb], 0, chunk(vb, c, be_r, bs_r, bc_r))),
                  pl.BlockSpec((1, 1, d, EXP_CE), lambda vb, c, be_r, bs_r, bc_r: (layer, be_r[vb], 0, EXP_NC + chunk(vb, c, be_r, bs_r, bc_r))),
                  pl.BlockSpec((1, 1, EXP_CE, d), lambda vb, c, be_r, bs_r, bc_r: (layer, be_r[vb], chunk(vb, c, be_r, bs_r, bc_r), 0))],
        out_specs=pl.BlockSpec(memory_space=pl.ANY),
        scratch_shapes=[pltpu.SMEM((1, EXP_SUB), jnp.int32),
                        pltpu.VMEM((EXP_SUB, d), F32),
                        pltpu.VMEM((EXP_BMAX, d), BF16),
                        pltpu.VMEM((d, 2 * EXP_CE), BF16),
                        pltpu.VMEM((EXP_CE, d), BF16),
                        pltpu.VMEM((EXP_BMAX, d), F32),
                        pltpu.SemaphoreType.DMA(()), pltpu.SemaphoreType.DMA(()), pltpu.SemaphoreType.DMA(())])
    return pl.pallas_call(
        _expert_kernel,
        grid_spec=grid_spec,
        out_shape=jax.ShapeDtypeStruct((n_slots, d), F32),
        compiler_params=_cparams(("arbitrary", "arbitrary")),
        name="routed_experts",
    )(be, bsub, bcnt, tok_slots, h, w_gu, w_gu, w_down)


def _ffn_gu_kernel(x_ref, wg_ref, wu_ref, o_ref):
    x = x_ref[...].astype(BF16)
    gate = jnp.dot(x, wg_ref[...], preferred_element_type=F32)
    up = jnp.dot(x, wu_ref[...], preferred_element_type=F32)
    o_ref[...] = (_silu(gate) * up).astype(o_ref.dtype)


def ffn_gu(x, w_gu, tm, tn):
    n, d = x.shape
    hdim = w_gu.shape[1] // 2
    nj = hdim // tn
    return pl.pallas_call(
        _ffn_gu_kernel,
        grid=(pl.cdiv(n, tm), nj),
        in_specs=[pl.BlockSpec((tm, d), lambda i, j: (i, 0)),
                  pl.BlockSpec((d, tn), lambda i, j: (0, j)),
                  pl.BlockSpec((d, tn), lambda i, j: (0, nj + j))],
        out_specs=pl.BlockSpec((tm, tn), lambda i, j: (i, j)),
        out_shape=jax.ShapeDtypeStruct((n, hdim), BF16),
        compiler_params=_cparams(("arbitrary", "arbitrary")),
        name="ffn_gu",
    )(x, w_gu, w_gu)


def _combine_kernel(slot_ref, w_ref, ys_hbm, sh_ref, x_ref, gt_ref, o_ref, buf, sem):
    nrow = buf.shape[0]
    tt = nrow // TOP_K

    def row_copy(j):
        return pltpu.make_async_copy(ys_hbm.at[pl.ds(slot_ref[0, 0, j], 1)], buf.at[pl.ds(j, 1)], sem)

    def issue(j, carry):
        row_copy(j).start()
        return carry

    def drain(j, carry):
        row_copy(j).wait()
        return carry

    lax.fori_loop(0, nrow, issue, 0)
    lax.fori_loop(0, nrow, drain, 0)
    acc = sh_ref[...]
    for k in range(TOP_K):
        acc = acc + w_ref[:, k:k + 1] * buf[k * tt:(k + 1) * tt, :]
    o_ref[...] = x_ref[...] + gt_ref[0] * acc


def moe_combine(slot_of, wts, ys, shared, x, gt, row0, rows_per_group):
    m, d = x.shape
    tt = COMB_T
    n = slot_of.shape[0]
    rb0 = row0 // tt
    slot_t = slot_of.reshape(n // tt, tt, TOP_K).transpose(0, 2, 1).reshape(n // tt, 1, TOP_K * tt)
    return pl.pallas_call(
        _combine_kernel,
        grid=(m // tt,),
        in_specs=[pl.BlockSpec((1, 1, TOP_K * tt), lambda i: (rb0 + i, 0, 0), memory_space=pltpu.SMEM),
                  pl.BlockSpec((tt, TOP_K), lambda i: (rb0 + i, 0)),
                  pl.BlockSpec(memory_space=pl.ANY),
                  pl.BlockSpec((tt, d), lambda i: (rb0 + i, 0)),
                  pl.BlockSpec((tt, d), lambda i: (i, 0)),
                  _row_mod_spec(gt, tt, rows_per_group)],
        out_specs=pl.BlockSpec((tt, d), lambda i: (i, 0)),
        out_shape=jax.ShapeDtypeStruct((m, d), F32),
        scratch_shapes=[pltpu.VMEM((TOP_K * tt, d), F32), pltpu.SemaphoreType.DMA(())],
        compiler_params=_cparams(("arbitrary",)),
        name="moe_combine",
    )(slot_t, wts, ys, shared, x, gt)


def _token_mixing(l, z4, bsz, t_real, t_seq, pos, a_caches, ret_s0, wkv_s0, shift_row, p):
    za, zb, zc, _ = z4
    prompt = a_caches is None

    def seq(z, t_to):
        z = z.reshape(bsz, t_real, z.shape[-1])
        return z if t_to == t_real else jnp.pad(z, ((0, 0), (0, t_to - t_real), (0, 0)))

    def rows(o):
        return o[:, :t_real].reshape(bsz * t_real, o.shape[-1])

    pos_seq = pos[0] + jnp.arange(t_seq, dtype=jnp.int32)
    tabs_a = rope_tables_a(pos_seq)
    tabs_b = rope_tables_b(pos_seq)
    za3, zb3 = seq(za, t_seq), seq(zb, t_seq)

    a_rows = []
    if prompt:
        oa = attn_prompt(za3, tabs_a)
        for g, (win, _) in enumerate(A_PATTERNS):
            nrow = min(win, t_seq)
            a_rows.append(kv_rows(za3, tabs_a, g, nrow, 128).reshape(bsz, nrow, 2, H_A, HD_A))
    else:
        nr = t_seq * H_A
        tabs_rows = tuple(jnp.repeat(tb, H_A, axis=0) for tb in tabs_a)
        z6 = za3.reshape(bsz, t_seq, 3, 3, H_A, HD_A)
        state = (jnp.zeros((bsz, nr, HD_A), F32), jnp.zeros((bsz, nr, HD_A), F32),
                 jnp.full((bsz, nr, HD_A), -jnp.inf, F32))
        for g in range(3):
            qkv = [z6[:, :, g, j].reshape(bsz, nr, HD_A) for j in range(3)]
            state = attn_sample(qkv, a_caches[g], l, tabs_rows, state, g, g == 2)
            kv = kv_rows(za3, tabs_a, g, t_seq, t_seq).reshape(bsz, t_seq, 2, H_A, HD_A)
            a_rows.append(kv[:, :t_real])
        oa = state[0].reshape(bsz, t_seq, A_OUT).astype(BF16)

    lc = RET_CHUNK if prompt else t_seq
    ob, ret_s = retention(zb3, tabs_b, retention_tables(lc, lc if prompt else t_real), ret_s0, lc)

    t_wkv = t_seq if prompt else WKV_L
    tt = 256 if prompt else WKV_L
    zc3 = seq(zc, t_wkv)
    prep = wkv_prep(zc3, shift_row, p, tt, None if prompt else t_real)
    y, wkv_s = wkv_chunk(prep, wkv_s0)
    oc = wkv_post(y, prep[7], prep[8], p['lnx_w'], p['lnx_b'], tt)
    shift_new = zc3[:, t_real - 1, :C_COLS]
    return (rows(oa), rows(ob), rows(oc)), (a_rows[0], a_rows[1], a_rows[2], ret_s, wkv_s, shift_new)


def kernel(x_prompt, x_sample, c_prompt, c_sample, cache_a0, cache_a1, cache_a2, state_ret, state_wkv,
           state_shift, w_ada, b_ada, norm1, norm2, w_in, mu_c, w_decay0, w_decay_up, a0, a_up, g_up, k_k,
           k_a, r_k, lnx_w, lnx_b, w_up_a, w_up_b, w_up_c, w_out, w_router, b_router, w_exp_gu, w_exp_down,
           w_sh_gu, w_sh_down, norm_f):
    bp, tp, d = x_prompt.shape
    bs, ts, _ = x_sample.shape
    n_layers = w_ada.shape[0]
    mp, ms = bp * tp, bs * ts
    n_tok = mp + ms
    assert ms == COMB_T and mp % COMB_T == 0
    n_blocks = N_EXPERTS + (n_tok * TOP_K) // EXP_BMAX + 1
    caches = (cache_a0, cache_a1, cache_a2)

    c_all = jnp.concatenate([c_prompt, c_sample, jnp.zeros((16 - bp - bs, d), F32)], axis=0)
    mod = ada_mod(c_all, w_ada, b_ada)

    xp = x_prompt.reshape(mp, d)
    xs = x_sample.reshape(ms, d)
    pos_p = jnp.arange(tp, dtype=jnp.int32)
    pos_s = PAST_LEN + jnp.arange(ts, dtype=jnp.int32)
    new_p = [[] for _ in range(6)]
    new_s = [[] for _ in range(6)]
    for l in range(n_layers):
        p = {'mu_c': mu_c[l], 'w_decay0': w_decay0[l], 'w_decay_up': w_decay_up[l], 'a0': a0[l], 'a_up': a_up[l],
             'g_up': g_up[l], 'k_k': k_k[l], 'k_a': k_a[l], 'r_k': r_k[l], 'lnx_w': lnx_w[l], 'lnx_b': lnx_b[l]}
        mods_p = [m.reshape(bp, 1, d) for m in jnp.split(mod[l, :bp], 6, axis=-1)]
        mods_s = [jnp.repeat(m, ts, axis=0).reshape(1, ms, d) for m in jnp.split(mod[l, bp:bp + bs], 6, axis=-1)]
        sh1p, sc1p, gt1p, sh2p, sc2p, gt2p = mods_p
        sh1s, sc1s, gt1s, sh2s, sc2s, gt2s = mods_s

        w_l = w_in[l]
        w_parts = (w_l[:, :A_COLS].astype(BF16),
                   w_l[:, A_COLS:A_COLS + B_COLS].astype(BF16),
                   jnp.pad(w_l[:, A_COLS + B_COLS:A_COLS + B_COLS + C_COLS], ((0, 0), (0, C_PAD - C_COLS))).astype(BF16),
                   w_l[:, A_COLS + B_COLS + C_COLS:].astype(BF16))
        tns = (1024, 1024, C_PAD // 3, 1024)
        hp = norm_mod(xp, norm1[l], sc1p, sh1p, 256, tp, BF16)
        hs = norm_mod(xs, norm1[l], sc1s, sh1s, ms, ms, BF16)
        z_p = [matmul(hp, w, 1024, tn) for w, tn in zip(w_parts, tns)]
        z_s = [matmul(hs, w, ms, tn) for w, tn in zip(w_parts, tns)]

        zeros_c = jnp.zeros((bp, 1, C_PAD), F32)
        o_p, st_p = _token_mixing(l, z_p, bp, tp, tp, pos_p, None,
                                  jnp.zeros((bp, H_B, DK_B, DV_B), F32), jnp.zeros((bp, H_C, HD_C, HD_C), F32),
                                  zeros_c, p)
        shift_s = jnp.pad(state_shift[l], ((0, 0), (0, C_PAD - C_COLS))).reshape(bs, 1, C_PAD)
        o_s, st_s = _token_mixing(l, z_s, bs, ts, SAMPLE_T_PAD, pos_s, caches, state_ret[l], state_wkv[l],
                                  shift_s, p)
        for lst, v in zip(new_p, st_p):
            lst.append(v)
        for lst, v in zip(new_s, st_s):
            lst.append(v)

        wa, wb, wc = w_up_a[l].astype(BF16), w_up_b[l].astype(BF16), w_up_c[l].astype(BF16)
        wo = w_out[l].astype(BF16)
        merged_p = upmerge(*o_p, wa, wb, wc, z_p[3], 1024, 512)
        merged_s = upmerge(*o_s, wa, wb, wc, z_s[3], ms, 512)
        x1p = proj_residual(merged_p, wo, xp, gt1p, 1024, 1024, tp)
        x1s = proj_residual(merged_s, wo, xs, gt1s, ms, 1024, ms)

        h2 = norm_mod(x1p, norm2[l], sc2p, sh2p, 256, tp, F32, out_rows=n_tok)
        h2 = norm_mod(x1s, norm2[l], sc2s, sh2s, ms, ms, F32, into=h2, into_row0=mp)
        idx, wts = router(h2, w_router[l], b_router[l], 256)
        tok_slots, slot_of, be, bsub, bcnt = route_plan(idx, n_blocks)
        ys = routed_experts(h2, tok_slots, be, bsub, bcnt, w_exp_gu, w_exp_down, l)
        hid = ffn_gu(h2, w_sh_gu[l].astype(BF16), 1024, 256)
        shared = matmul(hid, w_sh_down[l].astype(BF16), 1024, 1024)
        xp = moe_combine(slot_of, wts, ys, shared, x1p, gt2p, 0, tp)
        xs = moe_combine(slot_of, wts, ys, shared, x1s, gt2s, mp, ms)

    y_prompt = final_norm(xp, norm_f, 256).reshape(bp, tp, d)
    y_sample = final_norm(xs, norm_f, ms).reshape(bs, ts, d)
    outs_p = [jnp.stack(z) for z in new_p]
    outs_s = [jnp.stack(z) for z in new_s]
    return (y_prompt, y_sample, *outs_p, *outs_s)
```

```python
import functools
import math

import jax
import jax.numpy as jnp
import numpy as np
from jax import lax
from jax.experimental import pallas as pl
from jax.experimental.pallas import tpu as pltpu

F32 = jnp.float32
BF16 = jnp.bfloat16

D_MODEL = 4096
PAST_LEN = 16384
A_PATTERNS = ((128, 1), (512, 4), (2048, 16))
H_A = 8
HD_A = 128
A_OUT = H_A * HD_A
ROT_A = HD_A // 4
ROPE_THETA = 500000.0
A_COLS = 3 * 3 * A_OUT
H_B = 4
DK_B = 128
DV_B = 256
B_OUT = H_B * DV_B
B_COLS = 2 * H_B * DK_B + 2 * B_OUT
RET_THETA = 10000.0
RET_CHUNK = 128
H_C = 16
HD_C = 64
C_W = H_C * HD_C
LORA_W = 64
LORA_A = 64
LORA_G = 160
C_COLS = 3 * C_W + LORA_W + LORA_A + LORA_G
C_PAD = 3456
LORA_PAD = C_PAD - 3 * C_W
WKV_GN_EPS = 64e-5
N_EXPERTS = 128
N_EXP_GROUPS = 8
TOPK_GROUPS = 4
TOP_K = 8
D_EXPERT = 768
ROUTED_SCALE = 2.5
NORM_EPS = 1e-6
SAMPLE_T_PAD = 8

VMEM_LIMIT = 56 * 1024 * 1024
WKV_L = 32
WKV_G = 4


def _cparams(sem):
    return pltpu.CompilerParams(dimension_semantics=sem, vmem_limit_bytes=VMEM_LIMIT)


def _bdot(a, b):
    return jnp.dot(a.astype(BF16), b.astype(BF16), preferred_element_type=F32)


def _bdot_t(a, b):
    return lax.dot_general(a.astype(BF16), b.astype(BF16), (((1,), (1,)), ((), ())),
                           preferred_element_type=F32)


def _split3(a):
    a1 = a.astype(BF16)
    r1 = a - a1.astype(F32)
    a2 = r1.astype(BF16)
    a3 = (r1 - a2.astype(F32)).astype(BF16)
    return a1, a2, a3


def _dot_general3(a, b, dims):
    a1, a2, a3 = _split3(a)
    b1, b2, b3 = _split3(b)
    dg = functools.partial(lax.dot_general, dimension_numbers=dims, preferred_element_type=F32)
    small = dg(a1, b3) + dg(a3, b1) + dg(a2, b2)
    mid = dg(a1, b2) + dg(a2, b1)
    return dg(a1, b1) + (mid + small)


_NN = (((1,), (0,)), ((), ()))
_NT = (((1,), (1,)), ((), ()))
_TN = (((0,), (0,)), ((), ()))


def _silu(x):
    return x * jax.nn.sigmoid(x)


def _ada_kernel(c_ref, w_ref, b_ref, o_ref):
    o_ref[0] = _bdot(_silu(c_ref[...]), w_ref[0]) + b_ref[0]


def ada_mod(c_all, w_ada, b_ada):
    n_layers, d, n = w_ada.shape
    rows = c_all.shape[0]
    tn = 512
    return pl.pallas_call(
        _ada_kernel,
        grid=(n_layers, n // tn),
        in_specs=[pl.BlockSpec((rows, d), lambda l, j: (0, 0)),
                  pl.BlockSpec((1, d, tn), lambda l, j: (l, 0, j)),
                  pl.BlockSpec((1, 1, tn), lambda l, j: (l, 0, j))],
        out_specs=pl.BlockSpec((1, rows, tn), lambda l, j: (l, 0, j)),
        out_shape=jax.ShapeDtypeStruct((n_layers, rows, n), F32),
        compiler_params=_cparams(("arbitrary", "arbitrary")),
        name="ada_mod",
    )(c_all, w_ada, b_ada.reshape(n_layers, 1, n))


def _row_mod_spec(mod, tile, rows_per_group, tn=None):
    per_group = mod.shape[1] == 1
    lead = (lambda i: (i * tile // rows_per_group, 0)) if per_group else (lambda i: (0, i))
    rows = 1 if per_group else tile
    if tn is None:
        return pl.BlockSpec((1, rows, mod.shape[2]), lambda i: (*lead(i), 0))
    return pl.BlockSpec((1, rows, tn), lambda i, j: (*lead(i), j))


def _norm_mod_kernel(x_ref, g_ref, sc_ref, sh_ref, o_ref):
    x = x_ref[...]
    y = x * lax.rsqrt(jnp.mean(x * x, axis=-1, keepdims=True) + NORM_EPS) * g_ref[...]
    o_ref[...] = (y * (1.0 + sc_ref[0]) + sh_ref[0]).astype(o_ref.dtype)


def norm_mod(x2, g, sc, sh, tile, rows_per_group, out_dtype):
    m, d = x2.shape
    spec = _row_mod_spec(sc, tile, rows_per_group)
    return pl.pallas_call(
        _norm_mod_kernel,
        grid=(m // tile,),
        in_specs=[pl.BlockSpec((tile, d), lambda i: (i, 0)), pl.BlockSpec((1, d), lambda i: (0, 0)), spec, spec],
        out_specs=pl.BlockSpec((tile, d), lambda i: (i, 0)),
        out_shape=jax.ShapeDtypeStruct((m, d), out_dtype),
        compiler_params=_cparams(("arbitrary",)),
        name="norm_mod",
    )(x2, g.reshape(1, d), sc, sh)


def _final_norm_kernel(x_ref, g_ref, o_ref):
    x = x_ref[...]
    o_ref[...] = x * lax.rsqrt(jnp.mean(x * x, axis=-1, keepdims=True) + NORM_EPS) * g_ref[...]


def final_norm(x2, g, tile):
    m, d = x2.shape
    return pl.pallas_call(
        _final_norm_kernel,
        grid=(m // tile,),
        in_specs=[pl.BlockSpec((tile, d), lambda i: (i, 0)), pl.BlockSpec((1, d), lambda i: (0, 0))],
        out_specs=pl.BlockSpec((tile, d), lambda i: (i, 0)),
        out_shape=jax.ShapeDtypeStruct((m, d), F32),
        compiler_params=_cparams(("arbitrary",)),
        name="final_norm",
    )(x2, g.reshape(1, d))


def _mm_kernel(x_ref, w_ref, o_ref):
    o_ref[...] = _bdot(x_ref[...], w_ref[...]).astype(o_ref.dtype)


def matmul(x, w, tm, tn, out_dtype=F32, row0=0, rows=None):
    k = x.shape[1]
    rows = x.shape[0] - row0 if rows is None else rows
    n = w.shape[1]
    rb0 = row0 // tm
    return pl.pallas_call(
        _mm_kernel,
        grid=(pl.cdiv(rows, tm), n // tn),
        in_specs=[pl.BlockSpec((tm, k), lambda i, j: (rb0 + i, 0)),
                  pl.BlockSpec((k, tn), lambda i, j: (0, j))],
        out_specs=pl.BlockSpec((tm, tn), lambda i, j: (i, j)),
        out_shape=jax.ShapeDtypeStruct((rows, n), out_dtype),
        compiler_params=_cparams(("arbitrary", "arbitrary")),
        name="proj",
    )(x, w)


def rope_tables_a(pos):
    half = ROT_A // 2
    inv = ROPE_THETA ** (-jnp.arange(half, dtype=F32) / half)
    ang = pos.astype(F32)[:, None] * inv[None, :]
    cos, sin = jnp.cos(ang), jnp.sin(ang)
    t = pos.shape[0]
    one = jnp.ones((t, HD_A - ROT_A), F32)
    zero_tail = jnp.zeros((t, HD_A - ROT_A), F32)
    zero_h = jnp.zeros((t, half), F32)
    c = jnp.concatenate([cos, cos, one], axis=1)
    sp = jnp.concatenate([zero_h, sin, zero_tail], axis=1)
    sm = jnp.concatenate([-sin, zero_h, zero_tail], axis=1)
    return c, sp, sm


def rope_tables_b(pos):
    half = DK_B // 2
    inv = RET_THETA ** (-jnp.arange(half, dtype=F32) / half)
    ang = pos.astype(F32)[:, None] * inv[None, :]
    cos, sin = jnp.cos(ang), jnp.sin(ang)
    return jnp.concatenate([cos, cos], axis=1), jnp.concatenate([-sin, sin], axis=1)


def _rope_a(x, c, sp, sm):
    half = ROT_A // 2
    return x * c + pltpu.roll(x, half, 1) * sp + pltpu.roll(x, HD_A - half, 1) * sm


def _attn_prompt_kernel(q0, k0, v0, q1, k1, v1, q2, k2, v2, c_ref, sp_ref, sm_ref, o_ref,
                        qr, kr, og, mg, sg):
    t = qr.shape[0]
    blk = 128
    n_iter = t // blk
    qi = lax.broadcasted_iota(jnp.int32, (blk, blk), 0)
    ki = lax.broadcasted_iota(jnp.int32, (blk, blk), 1)
    cur_ok = qi >= ki
    prev_ok = ki >= qi
    neg = jnp.float32(-jnp.inf)
    scale = HD_A ** -0.5
    for g, (qf, kf, vf) in enumerate(((q0, k0, v0), (q1, k1, v1), (q2, k2, v2))):
        dil = A_PATTERNS[g][1]
        nb = t // (dil * blk)
        c, sp, sm = c_ref[...], sp_ref[...], sm_ref[...]
        qr[...] = _rope_a(qf[0], c, sp, sm) * scale
        kr[...] = _rope_a(kf[0], c, sp, sm)

        def body(it, carry, g=g, dil=dil, nb=nb, vf=vf):
            r = it % dil
            n = it // dil
            cur = pl.ds(n * (blk * dil) + r, blk, stride=dil)
            q = qr[cur, :]
            lc = jnp.where(cur_ok, _bdot_t(q, kr[cur, :]), neg)
            m = jnp.max(lc, axis=-1, keepdims=True)
            if nb > 1:
                prev = pl.ds(jnp.maximum(n - 1, 0) * (blk * dil) + r, blk, stride=dil)
                lp = jnp.where(prev_ok & (n > 0), _bdot_t(q, kr[prev, :]), neg)
                m = jnp.maximum(m, jnp.max(lp, axis=-1, keepdims=True))
            pc = jnp.exp(lc - m)
            s = jnp.sum(pc, axis=-1, keepdims=True)
            o = _bdot(pc, vf[0, cur, :])
            if nb > 1:
                pp = jnp.exp(lp - m)
                s = s + jnp.sum(pp, axis=-1, keepdims=True)
                o = o + _bdot(pp, vf[0, prev, :])
            og[g, cur, :] = o
            mg[g, cur, :] = jnp.broadcast_to(m, (blk, HD_A))
            sg[g, cur, :] = jnp.broadcast_to(s, (blk, HD_A))
            return carry

        lax.fori_loop(0, n_iter, body, 0, unroll=2)
    m_all = jnp.maximum(jnp.maximum(mg[0], mg[1]), mg[2])
    num = jnp.zeros_like(m_all)
    den = jnp.zeros_like(m_all)
    for g in range(3):
        e = jnp.exp(mg[g] - m_all)
        num = num + e * og[g]
        den = den + e * sg[g]
    o_ref[0] = (num / den).astype(o_ref.dtype)


def attn_prompt(za, tabs):
    b, t, _ = za.shape
    assert t % (A_PATTERNS[-1][1] * 128) == 0
    col_specs = []
    for g in range(3):
        for j in range(3):
            col_specs.append(pl.BlockSpec((1, t, HD_A), lambda bi, h, cb=(g * 3 + j) * H_A: (bi, 0, cb + h)))
    tab_spec = pl.BlockSpec((t, HD_A), lambda bi, h: (0, 0))
    return pl.pallas_call(
        _attn_prompt_kernel,
        grid=(b, H_A),
        in_specs=col_specs + [tab_spec] * 3,
        out_specs=pl.BlockSpec((1, t, HD_A), lambda bi, h: (bi, 0, h)),
        out_shape=jax.ShapeDtypeStruct((b, t, A_OUT), BF16),
        scratch_shapes=[pltpu.VMEM((t, HD_A), F32), pltpu.VMEM((t, HD_A), F32),
                        pltpu.VMEM((3, t, HD_A), F32), pltpu.VMEM((3, t, HD_A), F32),
                        pltpu.VMEM((3, t, HD_A), F32)],
        compiler_params=_cparams(("arbitrary", "arbitrary")),
        name="attn_prompt",
    )(*([za] * 9), *tabs)


def _kv_rows_kernel(k_ref, v_ref, c_ref, sp_ref, sm_ref, o_ref):
    rt = k_ref.shape[1]
    c, sp, sm = c_ref[...], sp_ref[...], sm_ref[...]
    for h in range(H_A):
        cols = slice(h * HD_A, (h + 1) * HD_A)
        o_ref[0, pl.ds(h, rt, stride=2 * H_A), :] = _rope_a(k_ref[0, :, cols], c, sp, sm)
        o_ref[0, pl.ds(H_A + h, rt, stride=2 * H_A), :] = v_ref[0, :, cols]


def kv_rows(za, tabs, g, rows, rt):
    b, t, _ = za.shape
    off = (t - rows) // rt
    tab_spec = pl.BlockSpec((rt, HD_A), lambda bi, i: (off + i, 0))
    return pl.pallas_call(
        _kv_rows_kernel,
        grid=(b, rows // rt),
        in_specs=[pl.BlockSpec((1, rt, A_OUT), lambda bi, i: (bi, off + i, g * 3 + 1)),
                  pl.BlockSpec((1, rt, A_OUT), lambda bi, i: (bi, off + i, g * 3 + 2)),
                  tab_spec, tab_spec, tab_spec],
        out_specs=pl.BlockSpec((1, rt * 2 * H_A, HD_A), lambda bi, i: (bi, i, 0)),
        out_shape=jax.ShapeDtypeStruct((b, rows * 2 * H_A, HD_A), F32),
        compiler_params=_cparams(("arbitrary", "arbitrary")),
        name="kv_rows",
    )(za, za, *tabs)


def _attn_sample_kernel(q_ref, k_ref, v_ref, cache_ref, c_ref, sp_ref, sm_ref, num_ref, den_ref, m_ref,
                        onum_ref, oden_ref, om_ref, *, dil, n_back, final):
    rows = q_ref.shape[1]
    lb = cache_ref.shape[2]
    c, sp, sm = c_ref[...], sp_ref[...], sm_ref[...]
    q = _rope_a(q_ref[0], c, sp, sm) * (HD_A ** -0.5)
    kn = _rope_a(k_ref[0], c, sp, sm)
    kc = cache_ref[0, 0, :, 0, :, :].reshape(lb * H_A, HD_A)
    vc = cache_ref[0, 0, :, 1, :, :].reshape(lb * H_A, HD_A)
    neg = jnp.float32(-jnp.inf)
    span = n_back * dil

    row = lax.broadcasted_iota(jnp.int32, (rows, lb * H_A), 0)
    col = lax.broadcasted_iota(jnp.int32, (rows, lb * H_A), 1)
    dist = lb + (row // H_A) - (col // H_A)
    ok = ((col % H_A) == (row % H_A)) & ((dist % dil) == 0) & (dist <= span)
    lc = jnp.where(ok, _bdot_t(q, kc), neg)

    rown = lax.broadcasted_iota(jnp.int32, (rows, rows), 0)
    coln = lax.broadcasted_iota(jnp.int32, (rows, rows), 1)
    dn = (rown // H_A) - (coln // H_A)
    okn = ((coln % H_A) == (rown % H_A)) & (dn >= 0) & ((dn % dil) == 0) & (dn <= span)
    ln = jnp.where(okn, _bdot_t(q, kn), neg)

    m = jnp.maximum(jnp.max(lc, axis=-1, keepdims=True), jnp.max(ln, axis=-1, keepdims=True))
    pc = jnp.exp(lc - m)
    pn = jnp.exp(ln - m)
    s = jnp.sum(pc, axis=-1, keepdims=True) + jnp.sum(pn, axis=-1, keepdims=True)
    o = _bdot(pc, vc) + _bdot(pn, v_ref[0])

    m_prev = m_ref[0]
    m_new = jnp.maximum(m_prev, m)
    e_prev = jnp.exp(m_prev - m_new)
    e_cur = jnp.exp(m - m_new)
    num = num_ref[0] * e_prev + o * e_cur
    den = den_ref[0] * e_prev + s * e_cur
    onum_ref[0] = num / den if final else num
    oden_ref[0] = den
    om_ref[0] = m_new


def attn_sample(qkv, cache, layer, tabs, state, g, final):
    b, rows, _ = qkv[0].shape
    lb = cache.shape[2]
    win, dil = A_PATTERNS[g]
    row_spec = pl.BlockSpec((1, rows, HD_A), lambda bi: (bi, 0, 0))
    tab_spec = pl.BlockSpec((rows, HD_A), lambda bi: (0, 0))
    st = jax.ShapeDtypeStruct((b, rows, HD_A), F32)
    return pl.pallas_call(
        functools.partial(_attn_sample_kernel, dil=dil, n_back=win // dil, final=final),
        grid=(b,),
        in_specs=[row_spec, row_spec, row_spec,
                  pl.BlockSpec((1, 1, lb, 2, H_A, HD_A), lambda bi: (layer, bi, 0, 0, 0, 0)),
                  tab_spec, tab_spec, tab_spec, row_spec, row_spec, row_spec],
        out_specs=[row_spec, row_spec, row_spec],
        out_shape=[st, st, st],
        compiler_params=_cparams(("arbitrary",)),
        name="attn_sample",
    )(*qkv, cache, *tabs, *state)


def retention_tables(lc, l_real):
    log_gamma = np.log(1.0 - 2.0 ** (-5.0 - np.arange(H_B, dtype=np.float32))).astype(np.float32)
    i = np.arange(lc, dtype=np.float32)
    diff = i[:, None] - i[None, :]
    real = (i < l_real)
    inner = np.where((diff >= 0) & real[None, :], np.exp(np.maximum(diff, 0.0)[None] * log_gamma[:, None, None]), 0.0)
    qdec = np.exp((i + 1.0)[None, :] * log_gamma[:, None])[:, :, None]
    kdec = np.where(real[None, :], np.exp((l_real - 1.0 - i)[None, :] * log_gamma[:, None]), 0.0)[:, :, None]
    gl = np.exp(l_real * log_gamma)[:, None, None]
    f = lambda z: jnp.asarray(z, F32)
    return f(inner), f(qdec), f(kdec), f(gl)


def _retention_kernel(q_ref, k_ref, v_ref, g_ref, c_ref, s_ref, dm_ref, qd_ref, kd_ref, gl_ref, s0_ref,
                      o_ref, so_ref, *, lc):
    t = q_ref.shape[1]
    so_ref[0, 0] = s0_ref[0, 0]
    dm = dm_ref[0]
    qd = qd_ref[0]
    kd = kd_ref[0]
    gl = gl_ref[0]

    def body(ci, carry):
        rows = pl.ds(pl.multiple_of(ci * lc, lc), lc)
        c, s = c_ref[rows, :], s_ref[rows, :]
        qx, kx = q_ref[0, rows, :], k_ref[0, rows, :]
        q = qx * c + pltpu.roll(qx, DK_B // 2, 1) * s
        k = (kx * c + pltpu.roll(kx, DK_B // 2, 1) * s) * (DK_B ** -0.5)
        v = v_ref[0, rows, :]
        st = so_ref[0, 0]
        o = _bdot(_bdot_t(q, k) * dm, v) + _bdot(q, st) * qd
        so_ref[0, 0] = st * gl + lax.dot_general((k * kd).astype(BF16), v.astype(BF16), _TN,
                                                 preferred_element_type=F32)
        o = o * lax.rsqrt(jnp.mean(o * o, axis=-1, keepdims=True) + NORM_EPS)
        o_ref[0, rows, :] = (_silu(g_ref[0, rows, :]) * o).astype(o_ref.dtype)
        return carry

    lax.fori_loop(0, t // lc, body, 0)


def retention(zb, tabs, dec, s0, lc):
    b, t, _ = zb.shape
    nqk = H_B * DK_B // DV_B
    tab_spec = pl.BlockSpec((t, DK_B), lambda bi, h: (0, 0))
    return pl.pallas_call(
        functools.partial(_retention_kernel, lc=lc),
        grid=(b, H_B),
        in_specs=[pl.BlockSpec((1, t, DK_B), lambda bi, h: (bi, 0, h)),
                  pl.BlockSpec((1, t, DK_B), lambda bi, h: (bi, 0, H_B + h)),
                  pl.BlockSpec((1, t, DV_B), lambda bi, h: (bi, 0, 2 * nqk + h)),
                  pl.BlockSpec((1, t, DV_B), lambda bi, h: (bi, 0, 2 * nqk + H_B + h)),
                  tab_spec, tab_spec,
                  pl.BlockSpec((1, lc, lc), lambda bi, h: (h, 0, 0)),
                  pl.BlockSpec((1, lc, 1), lambda bi, h: (h, 0, 0)),
                  pl.BlockSpec((1, lc, 1), lambda bi, h: (h, 0, 0)),
                  pl.BlockSpec((1, 1, 1), lambda bi, h: (h, 0, 0)),
                  pl.BlockSpec((1, 1, DK_B, DV_B), lambda bi, h: (bi, h, 0, 0))],
        out_specs=[pl.BlockSpec((1, t, DV_B), lambda bi, h: (bi, 0, h)),
                   pl.BlockSpec((1, 1, DK_B, DV_B), lambda bi, h: (bi, h, 0, 0))],
        out_shape=[jax.ShapeDtypeStruct((b, t, B_OUT), BF16),
                   jax.ShapeDtypeStruct((b, H_B, DK_B, DV_B), F32)],
        compiler_params=_cparams(("arbitrary", "arbitrary")),
        name="retention",
    )(zb, zb, zb, zb, *tabs, *dec, s0)


def _seg_sum(x, seg_ref):
    x1 = x.astype(BF16)
    x2 = (x - x1.astype(F32)).astype(BF16)
    seg = seg_ref[...]
    return jnp.dot(x1, seg, preferred_element_type=F32) + jnp.dot(x2, seg, preferred_element_type=F32)


def _wkv_prep_kernel(zc_ref, zp_ref, sh_ref, mu_ref, wl_ref, w0_ref, a0_ref, kk_ref, ka_ref, rk_ref,
                     seg_ref, tri_ref, ones_ref,
                     rt_ref, at_ref, bt_ref, kt_ref, bh_ref, kh_ref, v_ref, bon_ref, g_ref, wlast_ref,
                     *, t_valid):
    i = pl.program_id(1)
    tt = zc_ref.shape[1]
    zc = zc_ref[0]
    row = lax.broadcasted_iota(jnp.int32, (tt, 1), 0)
    first = jnp.where(i == 0, sh_ref[0], zp_ref[0, 7:8, :])
    prev = jnp.where(row == 0, first, pltpu.roll(zc, 1, 0))
    xc = zc + (prev - zc) * mu_ref[...]
    r = xc[:, 0:C_W]
    k = xc[:, C_W:2 * C_W]
    v = xc[:, 2 * C_W:3 * C_W]
    xl = xc[:, 3 * C_W:]
    lane = lax.broadcasted_iota(jnp.int32, xl.shape, 1)
    act = jnp.where(lane < LORA_W, jnp.tanh(xl),
                    jnp.where(lane < LORA_W + LORA_A, xl,
                              jnp.where(lane < LORA_W + LORA_A + LORA_G, jax.nn.sigmoid(xl), 0.0)))
    lora = _bdot(act, wl_ref[...])
    wl = w0_ref[...] + lora[:, 0:C_W]
    softplus = jnp.maximum(-wl, 0.0) + jnp.log(1.0 + jnp.exp(-jnp.abs(wl)))
    lw = -jnp.exp(-softplus - 0.5)
    a = jax.nn.sigmoid(a0_ref[...] + lora[:, C_W:2 * C_W])
    g_ref[0] = lora[:, 2 * C_W:3 * C_W]
    kk = k * kk_ref[...]
    kk = kk / jnp.maximum(jnp.sqrt(_seg_sum(kk * kk, seg_ref)), 1e-12)
    k2 = k * (1.0 + (a - 1.0) * ka_ref[...])
    bon_ref[0] = _seg_sum(r * k2 * rk_ref[...], seg_ref) * v
    al = -kk
    be = kk * a
    if t_valid is not None:
        ok = (i * tt + row) < t_valid
        lw = jnp.where(ok, lw, 0.0)
        al = jnp.where(ok, al, 0.0)
        be = jnp.where(ok, be, 0.0)
        k2 = jnp.where(ok, k2, 0.0)
    l1, l2, l3 = _split3(lw)
    tri = tri_ref[...]
    ones = ones_ref[...]
    dot = functools.partial(jnp.dot, preferred_element_type=F32)
    cum = dot(tri, l1) + (dot(tri, l2) + dot(tri, l3))
    tot = dot(ones, l1) + (dot(ones, l2) + dot(ones, l3))
    winv = jnp.exp(-cum)
    wrest = jnp.exp(tot - cum)
    rt_ref[0] = r * jnp.exp(cum)
    at_ref[0] = al * jnp.exp(cum - lw)
    bt_ref[0] = be * winv
    kt_ref[0] = k2 * winv
    bh_ref[0] = be * wrest
    kh_ref[0] = k2 * wrest
    v_ref[0] = v
    wl_all = jnp.exp(tot)
    for c in range(tt // WKV_L):
        wlast_ref[0, c:c + 1, :] = wl_all[c * WKV_L:c * WKV_L + 1, :]


def _blockdiag_ones(n, blk, lower):
    i = np.arange(n)
    same = (i[:, None] // blk) == (i[None, :] // blk)
    m = same & (i[:, None] >= i[None, :]) if lower else same
    return jnp.asarray(m, BF16)


def wkv_prep(zc, shift_row, p, tt, t_valid):
    b, t, _ = zc.shape
    row = lambda z: z.reshape(1, -1)
    mu = jnp.pad(p['mu_c'], (0, C_PAD - C_COLS)).reshape(1, C_PAD)
    w_lora = jnp.zeros((LORA_PAD, 3 * C_W), F32)
    w_lora = w_lora.at[0:LORA_W, 0:C_W].set(p['w_decay_up'])
    w_lora = w_lora.at[LORA_W:LORA_W + LORA_A, C_W:2 * C_W].set(p['a_up'])
    w_lora = w_lora.at[LORA_W + LORA_A:LORA_W + LORA_A + LORA_G, 2 * C_W:].set(p['g_up']).astype(BF16)
    seg = _blockdiag_ones(C_W, HD_C, False)
    tri = _blockdiag_ones(tt, WKV_L, True)
    ones = _blockdiag_ones(tt, WKV_L, False)
    vec = pl.BlockSpec((1, C_W), lambda bi, i: (0, 0))
    const = lambda shape: pl.BlockSpec(shape, lambda bi, i: (0, 0))
    blk = pl.BlockSpec((1, tt, C_W), lambda bi, i: (bi, i, 0))
    full = jax.ShapeDtypeStruct((b, t, C_W), F32)
    return pl.pallas_call(
        functools.partial(_wkv_prep_kernel, t_valid=t_valid),
        grid=(b, t // tt),
        in_specs=[pl.BlockSpec((1, tt, C_PAD), lambda bi, i: (bi, i, 0)),
                  pl.BlockSpec((1, 8, C_PAD), lambda bi, i: (bi, jnp.maximum(i * (tt // 8) - 1, 0), 0)),
                  pl.BlockSpec((1, 1, C_PAD), lambda bi, i: (bi, 0, 0)),
                  const((1, C_PAD)), const((LORA_PAD, 3 * C_W)), vec, vec, vec, vec, vec,
                  const((C_W, C_W)), const((tt, tt)), const((tt, tt))],
        out_specs=[blk] * 9 + [pl.BlockSpec((1, tt // WKV_L, C_W), lambda bi, i: (bi, i, 0))],
        out_shape=[full] * 9 + [jax.ShapeDtypeStruct((b, t // WKV_L, C_W), F32)],
        compiler_params=_cparams(("arbitrary", "arbitrary")),
        name="wkv_prep",
    )(zc, zc, shift_row, mu, w_lora, row(p['w_decay0']), row(p['a0']), row(p['k_k']), row(p['k_a']),
      row(p['r_k']), seg, tri, ones)


def _wkv_chunk_kernel(rt_ref, at_ref, bt_ref, kt_ref, bh_ref, kh_ref, v_ref, wl_ref, s0_ref,
                      mst_ref, pm_ref, eye_ref, y_ref, so_ref):
    tt = rt_ref.shape[1]
    n = WKV_G * WKV_L
    w = WKV_G * HD_C

    @pl.when(pl.program_id(1) == 0)
    def _init():
        so_ref[...] = s0_ref[...]

    mst = mst_ref[...]
    pm = pm_ref[...]
    eye = eye_ref[...]

    def stack(x):
        return (jnp.concatenate([x] * WKV_G, axis=0) * mst).astype(BF16)

    def body(ci, carry):
        rows = pl.ds(pl.multiple_of(ci * WKV_L, WKV_L), WKV_L)
        groups = range(H_C // WKV_G)
        lanes = [slice(hg * w, (hg + 1) * w) for hg in groups]
        ld = lambda ref, hg: stack(ref[0, rows, lanes[hg]])
        lhs = [jnp.concatenate([ld(at_ref, hg), ld(rt_ref, hg)], axis=0) for hg in groups]
        rhs = [jnp.concatenate([ld(bt_ref, hg), ld(kt_ref, hg)], axis=0) for hg in groups]
        vst = [ld(v_ref, hg) for hg in groups]
        s_bd = [so_ref[0, hg] for hg in groups]
        p = [_bdot_t(lhs[hg], rhs[hg]) * pm for hg in groups]
        q = [_bdot_t(lhs[hg], s_bd[hg]) for hg in groups]
        nil = [p[hg][:n, :n] for hg in groups]
        x = [eye + nil[hg] for hg in groups]
        for _ in range(int(math.log2(WKV_L)) - 1):
            nil = [_bdot(nil[hg], nil[hg]) for hg in groups]
            x = [x[hg] + _bdot(x[hg], nil[hg]) for hg in groups]
        rhs_u = [q[hg][:n] + _bdot(p[hg][:n, n:], vst[hg]) for hg in groups]
        u = [_bdot(x[hg], rhs_u[hg]) for hg in groups]
        uv = [jnp.concatenate([u[hg].astype(BF16), vst[hg]], axis=0) for hg in groups]
        y = [q[hg][n:] + _bdot(p[hg][n:], uv[hg]) for hg in groups]
        for hg in groups:
            acc = y[hg][0:WKV_L]
            for g in range(1, WKV_G):
                acc = acc + y[hg][g * WKV_L:(g + 1) * WKV_L]
            y_ref[0, rows, lanes[hg]] = acc
        hat = [jnp.concatenate([ld(bh_ref, hg), ld(kh_ref, hg)], axis=0) for hg in groups]
        upd = [lax.dot_general(uv[hg], hat[hg], _TN, preferred_element_type=F32) for hg in groups]
        for hg in groups:
            so_ref[0, hg] = s_bd[hg] * wl_ref[0, pl.ds(ci, 1), lanes[hg]] + upd[hg]
        return carry

    lax.fori_loop(0, tt // WKV_L, body, 0)


def wkv_chunk(prep, s0):
    rt, at, bt, kt, bh, kh, v, _, _, wlast = prep
    b, t, _ = rt.shape
    ng = H_C // WKV_G
    w = WKV_G * HD_C
    n = WKV_G * WKV_L
    s0g = s0.astype(F32).reshape(b, ng, WKV_G, HD_C, HD_C)
    eye_g = jnp.eye(WKV_G, dtype=F32)
    s_bd = jnp.einsum('bngij,gh->bngihj', s0g, eye_g).reshape(b, ng, w, w)
    mst = jnp.asarray((np.arange(n)[:, None] // WKV_L) == (np.arange(w)[None, :] // HD_C), F32)
    pi = np.arange(2 * n)
    pm_np = np.where((pi[:, None] < n), (pi[:, None] % n) > (pi[None, :] % n), (pi[:, None] % n) >= (pi[None, :] % n))
    pm = jnp.asarray(pm_np, F32)
    eye = jnp.eye(n, dtype=F32)
    tt = min(t, 512)
    blk = pl.BlockSpec((1, tt, C_W), lambda bi, ti: (bi, ti, 0))
    const = lambda shape: pl.BlockSpec(shape, lambda bi, ti: (0, 0))
    st_spec = pl.BlockSpec((1, ng, w, w), lambda bi, ti: (bi, 0, 0, 0))
    y, s_new = pl.pallas_call(
        _wkv_chunk_kernel,
        grid=(b, t // tt),
        in_specs=[blk] * 7 + [pl.BlockSpec((1, tt // WKV_L, C_W), lambda bi, ti: (bi, ti, 0)), st_spec,
                              const((n, w)), const((2 * n, 2 * n)), const((n, n))],
        out_specs=[blk, st_spec],
        out_shape=[jax.ShapeDtypeStruct((b, t, C_W), F32), jax.ShapeDtypeStruct((b, ng, w, w), F32)],
        compiler_params=_cparams(("arbitrary", "arbitrary")),
        name="wkv_chunk",
    )(rt, at, bt, kt, bh, kh, v, wlast, s_bd, mst, pm, eye)
    s_new = jnp.einsum('bngihj,gh->bngij', s_new.reshape(b, ng, WKV_G, HD_C, WKV_G, HD_C), eye_g)
    return y, s_new.reshape(b, H_C, HD_C, HD_C)


def _wkv_post_kernel(y_ref, bon_ref, g_ref, lw_ref, lb_ref, seg_ref, o_ref):
    y = y_ref[0]
    mean = _seg_sum(y, seg_ref) * (1.0 / HD_C)
    d = y - mean
    var = _seg_sum(d * d, seg_ref) * (1.0 / HD_C)
    yn = d * lax.rsqrt(var + WKV_GN_EPS) * lw_ref[...] + lb_ref[...]
    o_ref[0] = ((yn + bon_ref[0]) * g_ref[0]).astype(o_ref.dtype)


def wkv_post(y, bon, g, lnx_w, lnx_b, tt):
    b, t, _ = y.shape
    blk = pl.BlockSpec((1, tt, C_W), lambda bi, i: (bi, i, 0))
    vec = pl.BlockSpec((1, C_W), lambda bi, i: (0, 0))
    return pl.pallas_call(
        _wkv_post_kernel,
        grid=(b, t // tt),
        in_specs=[blk, blk, blk, vec, vec, pl.BlockSpec((C_W, C_W), lambda bi, i: (0, 0))],
        out_specs=blk,
        out_shape=jax.ShapeDtypeStruct((b, t, C_W), BF16),
        compiler_params=_cparams(("arbitrary", "arbitrary")),
        name="wkv_post",
    )(y, bon, g, lnx_w.reshape(1, C_W), lnx_b.reshape(1, C_W), _blockdiag_ones(C_W, HD_C, False))


def _upmerge_kernel(oa_ref, ob_ref, oc_ref, wa_ref, wb_ref, wc_ref, g0_ref, g1_ref, g2_ref, o_ref):
    acc = jax.nn.sigmoid(g0_ref[...]) * jnp.dot(oa_ref[...], wa_ref[...], preferred_element_type=F32)
    acc = acc + jax.nn.sigmoid(g1_ref[...]) * jnp.dot(ob_ref[...], wb_ref[...], preferred_element_type=F32)
    acc = acc + jax.nn.sigmoid(g2_ref[...]) * jnp.dot(oc_ref[...], wc_ref[...], preferred_element_type=F32)
    o_ref[...] = acc.astype(o_ref.dtype)


def upmerge(oa, ob, oc, wa, wb, wc, zg, tm, tn):
    m = oa.shape[0]
    d = wa.shape[1]
    nj = d // tn
    act = pl.BlockSpec((tm, oa.shape[1]), lambda i, j: (i, 0))
    wsp = pl.BlockSpec((wa.shape[0], tn), lambda i, j: (0, j))
    gate = lambda br: pl.BlockSpec((tm, tn), lambda i, j: (i, br * nj + j))
    return pl.pallas_call(
        _upmerge_kernel,
        grid=(m // tm, nj),
        in_specs=[act, act, act, wsp, wsp, wsp, gate(0), gate(1), gate(2)],
        out_specs=pl.BlockSpec((tm, tn), lambda i, j: (i, j)),
        out_shape=jax.ShapeDtypeStruct((m, d), BF16),
        compiler_params=_cparams(("arbitrary", "arbitrary")),
        name="upmerge",
    )(oa, ob, oc, wa, wb, wc, zg, zg, zg)


def _proj_res_kernel(a_ref, w_ref, x_ref, gt_ref, o_ref):
    o_ref[...] = x_ref[...] + gt_ref[0] * _bdot(a_ref[...], w_ref[...])


def proj_residual(a, w, x, gt, tm, tn, rows_per_group):
    m, k = a.shape
    n = w.shape[1]
    return pl.pallas_call(
        _proj_res_kernel,
        grid=(m // tm, n // tn),
        in_specs=[pl.BlockSpec((tm, k), lambda i, j: (i, 0)),
                  pl.BlockSpec((k, tn), lambda i, j: (0, j)),
                  pl.BlockSpec((tm, tn), lambda i, j: (i, j)),
                  _row_mod_spec(gt, tm, rows_per_group, tn)],
        out_specs=pl.BlockSpec((tm, tn), lambda i, j: (i, j)),
        out_shape=jax.ShapeDtypeStruct((m, n), F32),
        compiler_params=_cparams(("arbitrary", "arbitrary")),
        name="proj_residual",
    )(a, w, x, gt)


EXP_SUB = 128
EXP_NSUB = 5
EXP_BMAX = EXP_SUB * EXP_NSUB
EXP_CE = 256
EXP_NC = D_EXPERT // EXP_CE
EXP_KSLABS = 4
EXP_JSLABS = 4
COMB_T = 32
COMB_ROWS = COMB_T * TOP_K


def _router_kernel(h_ref, w_ref, b_ref, tri_ref, idx_ref, wts_ref, rank_ref, cnt_ref, carry, *, n_valid):
    h1, h2, _ = _split3(h_ref[...])
    w1, w2, _ = _split3(w_ref[...])
    dot = functools.partial(jnp.dot, preferred_element_type=F32)
    logits = dot(h1, w1) + (dot(h1, w2) + dot(h2, w1))
    scores = jax.nn.sigmoid(logits)
    choice = scores + b_ref[...]
    tm = choice.shape[0]
    lane = lax.broadcasted_iota(jnp.int32, (tm, N_EXPERTS), 1)
    per_group = N_EXPERTS // N_EXP_GROUPS
    grp = lane // per_group
    neg = jnp.float32(-jnp.inf)

    def first_argmax(x, ids, sentinel):
        m = jnp.max(x, axis=-1, keepdims=True)
        return m, jnp.min(jnp.where(x == m, ids, sentinel), axis=-1, keepdims=True)

    gscore = jnp.zeros_like(choice)
    for g in range(N_EXP_GROUPS):
        xg = jnp.where(grp == g, choice, neg)
        m1, i1 = first_argmax(xg, lane, N_EXPERTS)
        m2 = jnp.max(jnp.where(lane == i1, neg, xg), axis=-1, keepdims=True)
        gscore = jnp.where(grp == g, m1 + m2, gscore)
    keep = jnp.zeros(choice.shape, jnp.bool_)
    for _ in range(TOPK_GROUPS):
        _, gi = first_argmax(gscore, grp, N_EXP_GROUPS)
        keep = keep | (grp == gi)
        gscore = jnp.where(grp == gi, neg, gscore)
    masked = jnp.where(keep, choice, neg)
    lane_k = lax.broadcasted_iota(jnp.int32, (tm, TOP_K), 1)
    idx = jnp.zeros((tm, TOP_K), jnp.int32)
    wts = jnp.zeros((tm, TOP_K), F32)
    hits = []
    for k in range(TOP_K):
        _, ik = first_argmax(masked, lane, N_EXPERTS)
        hit = lane == ik
        hits.append(hit)
        wk = jnp.sum(jnp.where(hit, scores, 0.0), axis=-1, keepdims=True)
        masked = jnp.where(hit, neg, masked)
        idx = jnp.where(lane_k == k, ik, idx)
        wts = jnp.where(lane_k == k, wk, wts)
    wts_ref[...] = wts / jnp.sum(wts, axis=-1, keepdims=True) * ROUTED_SCALE
    idx_ref[...] = idx

    @pl.when(pl.program_id(0) == 0)
    def _init():
        carry[...] = jnp.zeros_like(carry)

    row = pl.program_id(0) * tm + lax.broadcasted_iota(jnp.int32, (tm, 1), 0)
    sel = jnp.zeros(choice.shape, F32)
    for hit in hits:
        sel = sel + jnp.where(hit, 1.0, 0.0)
    sel = jnp.where(row < n_valid, sel, 0.0)
    before = jnp.dot(tri_ref[...], sel.astype(BF16), preferred_element_type=F32) + carry[...]
    rank = jnp.zeros((tm, TOP_K), jnp.int32)
    for k, hit in enumerate(hits):
        rk = jnp.sum(jnp.where(hit, before, 0.0), axis=-1, keepdims=True)
        rank = jnp.where(lane_k == k, rk.astype(jnp.int32), rank)
    rank_ref[...] = rank
    carry[...] = carry[...] + jnp.sum(sel, axis=0, keepdims=True)
    cnt_ref[...] = carry[...].astype(jnp.int32)


def router(h, w_router, b_router, tm):
    n, d = h.shape
    tri = jnp.asarray(np.arange(tm)[:, None] > np.arange(tm)[None, :], BF16)
    tok = pl.BlockSpec((tm, TOP_K), lambda i: (i, 0))
    return pl.pallas_call(
        functools.partial(_router_kernel, n_valid=n),
        grid=(pl.cdiv(n, tm),),
        in_specs=[pl.BlockSpec((tm, d), lambda i: (i, 0)),
                  pl.BlockSpec((d, N_EXPERTS), lambda i: (0, 0)),
                  pl.BlockSpec((1, N_EXPERTS), lambda i: (0, 0)),
                  pl.BlockSpec((tm, tm), lambda i: (0, 0))],
        out_specs=[tok, tok, tok, pl.BlockSpec((1, N_EXPERTS), lambda i: (0, 0))],
        out_shape=[jax.ShapeDtypeStruct((n, TOP_K), jnp.int32), jax.ShapeDtypeStruct((n, TOP_K), F32),
                   jax.ShapeDtypeStruct((n, TOP_K), jnp.int32), jax.ShapeDtypeStruct((1, N_EXPERTS), jnp.int32)],
        scratch_shapes=[pltpu.VMEM((1, N_EXPERTS), F32)],
        compiler_params=_cparams(("arbitrary",)),
        name="router",
    )(h, w_router, b_router.reshape(1, N_EXPERTS), tri)


def _pack_bf16_pair(y):
    half = y.shape[1] // 2
    bits = lax.bitcast_convert_type(y.astype(BF16).astype(F32), jnp.uint32)
    return (bits[:, :half] >> 16) | (bits[:, half:] & jnp.uint32(0xFFFF0000))


def _unpack_bf16_pair(u):
    lo = lax.bitcast_convert_type(u << 16, F32).astype(BF16)
    hi = lax.bitcast_convert_type(u & jnp.uint32(0xFFFF0000), F32).astype(BF16)
    return lo, hi


def _norm_mod_moe_kernel(xp_ref, xs_ref, g_ref, scp_ref, shp_ref, scs_ref, shs_ref, h_ref, hp_ref, *, n_tiles):
    i = pl.program_id(0)

    def emit(x, sc, sh):
        rows = x.shape[0]
        y = x * lax.rsqrt(jnp.mean(x * x, axis=-1, keepdims=True) + NORM_EPS) * g_ref[...]
        y = y * (1.0 + sc) + sh
        h_ref[0:rows, :] = y
        hp_ref[0:rows, :] = _pack_bf16_pair(y)

    @pl.when(i < n_tiles)
    def _prompt():
        emit(xp_ref[...], scp_ref[0], shp_ref[0])

    @pl.when(i == n_tiles)
    def _sample():
        emit(xs_ref[...], scs_ref[0], shs_ref[0])


def norm_mod_moe(xp, xs, g, scp, shp, scs, shs, tile, rows_per_group):
    mp, d = xp.shape
    ms = xs.shape[0]
    n_tiles = mp // tile
    clamp = lambda i: jnp.minimum(i, n_tiles - 1)
    pspec = pl.BlockSpec((1, 1, d), lambda i: (clamp(i) * tile // rows_per_group, 0, 0))
    sspec = pl.BlockSpec((1, ms, d), lambda i: (0, 0, 0))
    return pl.pallas_call(
        functools.partial(_norm_mod_moe_kernel, n_tiles=n_tiles),
        grid=(n_tiles + 1,),
        in_specs=[pl.BlockSpec((tile, d), lambda i: (clamp(i), 0)),
                  pl.BlockSpec((ms, d), lambda i: (0, 0)),
                  pl.BlockSpec((1, d), lambda i: (0, 0)), pspec, pspec, sspec, sspec],
        out_specs=[pl.BlockSpec((tile, d), lambda i: (i, 0)), pl.BlockSpec((tile, d // 2), lambda i: (i, 0))],
        out_shape=[jax.ShapeDtypeStruct((mp + ms, d), F32), jax.ShapeDtypeStruct((mp + ms, d // 2), jnp.uint32)],
        compiler_params=_cparams(("arbitrary",)),
        name="norm_mod_moe",
    )(xp, xs, g.reshape(1, d), scp, shp, scs, shs)


def route_tables(idx, rank, counts, n_blocks):
    n = idx.shape[0]
    counts = counts.reshape(N_EXPERTS)
    padded = (counts + EXP_SUB - 1) // EXP_SUB * EXP_SUB
    pad_end = jnp.cumsum(padded)
    pad_start = pad_end - padded
    onehot = idx[:, :, None] == jnp.arange(N_EXPERTS, dtype=jnp.int32)[None, None, :]
    slot_of = rank + jnp.sum(jnp.where(onehot, pad_start[None, None, :], 0), axis=-1)
    tok = jnp.arange(n, dtype=jnp.int32)[:, None]
    kk = jnp.arange(TOP_K, dtype=jnp.int32)[None, :]
    dst_of = (tok // COMB_T) * COMB_ROWS + kk * COMB_T + tok % COMB_T
    n_sub = (n * TOP_K) // EXP_SUB + N_EXPERTS + EXP_NSUB - 1
    dst_slots = jnp.zeros((n_sub * EXP_SUB,), jnp.int32).at[slot_of.reshape(-1)].set(dst_of.reshape(-1))
    nblk = (counts + EXP_BMAX - 1) // EXP_BMAX
    blk_end = jnp.cumsum(nblk)
    bid = jnp.arange(n_blocks, dtype=jnp.int32)
    used = bid < blk_end[-1]
    be = jnp.sum((bid[:, None] >= blk_end[None, :]).astype(jnp.int32), axis=1)
    be = jnp.minimum(be, N_EXPERTS - 1)
    sel = be[:, None] == jnp.arange(N_EXPERTS, dtype=jnp.int32)[None, :]
    pick = lambda v: jnp.sum(jnp.where(sel, v[None, :], 0), axis=1)
    within = bid - pick(blk_end - nblk)
    bcnt = jnp.where(used, jnp.clip(pick(counts) - within * EXP_BMAX, 0, EXP_BMAX), 0).astype(jnp.int32)
    bsub = jnp.where(used, (pick(pad_start) + within * EXP_BMAX) // EXP_SUB, 0).astype(jnp.int32)
    last_e = jnp.sum(jnp.where(bid == blk_end[-1] - 1, be, 0))
    be = jnp.where(used, be, last_e).astype(jnp.int32)
    return dst_slots.reshape(n_sub, EXP_SUB), be, bsub, bcnt


ROW_UNROLL = 8


def _for_rows(n_rows, fn):
    n_main = n_rows // ROW_UNROLL

    def group(i, carry):
        for j in range(ROW_UNROLL):
            fn(i * ROW_UNROLL + j)
        return carry

    def single(r, carry):
        fn(r)
        return carry

    lax.fori_loop(0, n_main, group, 0)
    lax.fori_loop(n_main * ROW_UNROLL, n_rows, single, 0)


def _expert_kernel(be_ref, bsub_ref, bcnt_ref, dst_hbm, hp_hbm, wg_ref, wu_ref, wd_ref, ys_hbm,
                   dst_smem, xu, xb, wgu_b, wd_b, acc, sem_dst, sem_in, sem_out):
    vb = pl.program_id(0)
    c = pl.program_id(1)
    nb = pl.num_programs(0)
    cnt = bcnt_ref[vb]
    prev_cnt = jnp.where(vb > 0, bcnt_ref[jnp.maximum(vb - 1, 0)], 0)
    nxt = jnp.minimum(vb + 1, nb - 1)
    next_cnt = jnp.where(vb + 1 < nb, bcnt_ref[nxt], 0)
    cur = vb % 2
    half = xu.shape[1]

    def dst_copy(block, buf):
        return pltpu.make_async_copy(dst_hbm.at[pl.ds(bsub_ref[block], EXP_NSUB)], dst_smem.at[buf], sem_dst)

    sub_bits = EXP_SUB.bit_length() - 1
    t_bits = COMB_T.bit_length() - 1
    rows_bits = COMB_ROWS.bit_length() - 1

    def dst_at(buf, r):
        return dst_smem[buf, r >> sub_bits, r & (EXP_SUB - 1)]

    def gather_start(buf, n_rows):
        def issue(r):
            d = dst_at(buf, r)
            tok = ((d >> rows_bits) << t_bits) | (d & (COMB_T - 1))
            pltpu.make_async_copy(hp_hbm.at[pl.ds(tok, 1)], xu.at[pl.ds(r, 1)], sem_in).start()
        _for_rows(n_rows, issue)

    def gather_wait(n_rows):
        _for_rows(n_rows, lambda r: pltpu.make_async_copy(
            hp_hbm.at[pl.ds(0, 1)], xu.at[pl.ds(0, 1)], sem_in).wait())

    def scatter_wait(n_rows):
        _for_rows(n_rows, lambda r: pltpu.make_async_copy(
            acc.at[pl.ds(0, 1)], ys_hbm.at[pl.ds(0, 1)], sem_out).wait())

    @pl.when((vb == 0) & (c == 0))
    def _prologue():
        xu[...] = jnp.zeros_like(xu)
        cp = dst_copy(0, 0)
        cp.start()
        cp.wait()
        gather_start(0, cnt)

    @pl.when(cnt > 0)
    def _block():
        @pl.when(c == 0)
        def _arrive():
            gather_wait(cnt)
            lo, hi = _unpack_bf16_pair(xu[...])
            xb[:, :half] = lo
            xb[:, half:] = hi
            scatter_wait(prev_cnt)

            @pl.when(next_cnt > 0)
            def _():
                dst_copy(nxt, 1 - cur).start()

        @pl.when((c == 1) & (next_cnt > 0))
        def _prefetch():
            dst_copy(nxt, 1 - cur).wait()
            gather_start(1 - cur, next_cnt)

        n_k = wgu_b.shape[0]
        kw = wgu_b.shape[1]
        n_j = wd_b.shape[0]
        jw = wd_b.shape[2]

        def cast_gu(kh):
            rows = slice(kh * kw, (kh + 1) * kw)
            wgu_b[kh, :, :EXP_CE] = wg_ref[0, 0, rows, :].astype(BF16)
            wgu_b[kh, :, EXP_CE:] = wu_ref[0, 0, rows, :].astype(BF16)

        def cast_down(j):
            wd_b[j] = wd_ref[0, 0, :, j * jw:(j + 1) * jw].astype(BF16)

        cast_gu(0)
        hid = None
        for kh in range(n_k):
            if kh + 1 < n_k:
                cast_gu(kh + 1)
            else:
                cast_down(0)
            part = jnp.dot(xb[:, kh * kw:(kh + 1) * kw], wgu_b[kh], preferred_element_type=F32)
            hid = part if hid is None else hid + part
        act = (_silu(hid[:, :EXP_CE]) * hid[:, EXP_CE:]).astype(BF16)
        first = c == 0
        for j in range(n_j):
            if j + 1 < n_j:
                cast_down(j + 1)
            cols = slice(j * jw, (j + 1) * jw)
            yj = jnp.dot(act, wd_b[j], preferred_element_type=F32)
            acc[:, cols] = jnp.where(first, yj, acc[:, cols] + yj)

        @pl.when(c == EXP_NC - 1)
        def _write():
            _for_rows(cnt, lambda r: pltpu.make_async_copy(
                acc.at[pl.ds(r, 1)], ys_hbm.at[pl.ds(dst_at(cur, r), 1)], sem_out).start())

            @pl.when(next_cnt == 0)
            def _():
                scatter_wait(cnt)


def routed_experts(hp, dst_slots, be, bsub, bcnt, w_gu, w_down, layer):
    n, half = hp.shape
    d = 2 * half
    n_blocks = be.shape[0]

    def chunk(vb, c, bc_r):
        return jnp.where(bc_r[vb] > 0, c, EXP_NC - 1)

    grid_spec = pltpu.PrefetchScalarGridSpec(
        num_scalar_prefetch=3,
        grid=(n_blocks, EXP_NC),
        in_specs=[pl.BlockSpec(memory_space=pl.ANY),
                  pl.BlockSpec(memory_space=pl.ANY),
                  pl.BlockSpec((1, 1, d, EXP_CE), lambda vb, c, be_r, bs_r, bc_r: (layer, be_r[vb], 0, chunk(vb, c, bc_r))),
                  pl.BlockSpec((1, 1, d, EXP_CE), lambda vb, c, be_r, bs_r, bc_r: (layer, be_r[vb], 0, EXP_NC + chunk(vb, c, bc_r))),
                  pl.BlockSpec((1, 1, EXP_CE, d), lambda vb, c, be_r, bs_r, bc_r: (layer, be_r[vb], chunk(vb, c, bc_r), 0))],
        out_specs=pl.BlockSpec(memory_space=pl.ANY),
        scratch_shapes=[pltpu.SMEM((2, EXP_NSUB, EXP_SUB), jnp.int32),
                        pltpu.VMEM((EXP_BMAX, half), jnp.uint32),
                        pltpu.VMEM((EXP_BMAX, d), BF16),
                        pltpu.VMEM((EXP_KSLABS, d // EXP_KSLABS, 2 * EXP_CE), BF16),
                        pltpu.VMEM((EXP_JSLABS, EXP_CE, d // EXP_JSLABS), BF16),
                        pltpu.VMEM((EXP_BMAX, d), F32),
                        pltpu.SemaphoreType.DMA(()), pltpu.SemaphoreType.DMA(()), pltpu.SemaphoreType.DMA(())])
    return pl.pallas_call(
        _expert_kernel,
        grid_spec=grid_spec,
        out_shape=jax.ShapeDtypeStruct((n * TOP_K, d), F32),
        compiler_params=_cparams(("arbitrary", "arbitrary")),
        name="routed_experts",
    )(be, bsub, bcnt, dst_slots, hp, w_gu, w_gu, w_down)


def _ffn_gu_kernel(x_ref, wg_ref, wu_ref, o_ref):
    x = x_ref[...].astype(BF16)
    gate = jnp.dot(x, wg_ref[...], preferred_element_type=F32)
    up = jnp.dot(x, wu_ref[...], preferred_element_type=F32)
    o_ref[...] = (_silu(gate) * up).astype(o_ref.dtype)


def ffn_gu(x, w_gu, tm, tn):
    n, d = x.shape
    hdim = w_gu.shape[1] // 2
    nj = hdim // tn
    return pl.pallas_call(
        _ffn_gu_kernel,
        grid=(pl.cdiv(n, tm), nj),
        in_specs=[pl.BlockSpec((tm, d), lambda i, j: (i, 0)),
                  pl.BlockSpec((d, tn), lambda i, j: (0, j)),
                  pl.BlockSpec((d, tn), lambda i, j: (0, nj + j))],
        out_specs=pl.BlockSpec((tm, tn), lambda i, j: (i, j)),
        out_shape=jax.ShapeDtypeStruct((n, hdim), BF16),
        compiler_params=_cparams(("arbitrary", "arbitrary")),
        name="ffn_gu",
    )(x, w_gu, w_gu)


def _combine_kernel(w_ref, ys_ref, sh_ref, x_ref, gt_ref, o_ref):
    tt = w_ref.shape[0]
    acc = sh_ref[...]
    for k in range(TOP_K):
        acc = acc + w_ref[:, k:k + 1] * ys_ref[k * tt:(k + 1) * tt, :]
    o_ref[...] = x_ref[...] + gt_ref[0] * acc


def moe_combine(wts, ys, shared, x, gt, row0, rows_per_group):
    m, d = x.shape
    tt = COMB_T
    rb0 = row0 // tt
    return pl.pallas_call(
        _combine_kernel,
        grid=(m // tt,),
        in_specs=[pl.BlockSpec((tt, TOP_K), lambda i: (rb0 + i, 0)),
                  pl.BlockSpec((TOP_K * tt, d), lambda i: (rb0 + i, 0)),
                  pl.BlockSpec((tt, d), lambda i: (rb0 + i, 0)),
                  pl.BlockSpec((tt, d), lambda i: (i, 0)),
                  _row_mod_spec(gt, tt, rows_per_group)],
        out_specs=pl.BlockSpec((tt, d), lambda i: (i, 0)),
        out_shape=jax.ShapeDtypeStruct((m, d), F32),
        compiler_params=_cparams(("arbitrary",)),
        name="moe_combine",
    )(wts, ys, shared, x, gt)


def _token_mixing(l, z4, bsz, t_real, t_seq, pos, a_caches, ret_s0, wkv_s0, shift_row, p):
    za, zb, zc, _ = z4
    prompt = a_caches is None

    def seq(z, t_to):
        z = z.reshape(bsz, t_real, z.shape[-1])
        return z if t_to == t_real else jnp.pad(z, ((0, 0), (0, t_to - t_real), (0, 0)))

    def rows(o):
        return o[:, :t_real].reshape(bsz * t_real, o.shape[-1])

    pos_seq = pos[0] + jnp.arange(t_seq, dtype=jnp.int32)
    tabs_a = rope_tables_a(pos_seq)
    tabs_b = rope_tables_b(pos_seq)
    za3, zb3 = seq(za, t_seq), seq(zb, t_seq)

    a_rows = []
    if prompt:
        oa = attn_prompt(za3, tabs_a)
        for g, (win, _) in enumerate(A_PATTERNS):
            nrow = min(win, t_seq)
            a_rows.append(kv_rows(za3, tabs_a, g, nrow, 128).reshape(bsz, nrow, 2, H_A, HD_A))
    else:
        nr = t_seq * H_A
        tabs_rows = tuple(jnp.repeat(tb, H_A, axis=0) for tb in tabs_a)
        z6 = za3.reshape(bsz, t_seq, 3, 3, H_A, HD_A)
        state = (jnp.zeros((bsz, nr, HD_A), F32), jnp.zeros((bsz, nr, HD_A), F32),
                 jnp.full((bsz, nr, HD_A), -jnp.inf, F32))
        for g in range(3):
            qkv = [z6[:, :, g, j].reshape(bsz, nr, HD_A) for j in range(3)]
            state = attn_sample(qkv, a_caches[g], l, tabs_rows, state, g, g == 2)
            kv = kv_rows(za3, tabs_a, g, t_seq, t_seq).reshape(bsz, t_seq, 2, H_A, HD_A)
            a_rows.append(kv[:, :t_real])
        oa = state[0].reshape(bsz, t_seq, A_OUT).astype(BF16)

    lc = RET_CHUNK if prompt else t_seq
    ob, ret_s = retention(zb3, tabs_b, retention_tables(lc, lc if prompt else t_real), ret_s0, lc)

    t_wkv = t_seq if prompt else WKV_L
    tt = 256 if prompt else WKV_L
    zc3 = seq(zc, t_wkv)
    prep = wkv_prep(zc3, shift_row, p, tt, None if prompt else t_real)
    y, wkv_s = wkv_chunk(prep, wkv_s0)
    oc = wkv_post(y, prep[7], prep[8], p['lnx_w'], p['lnx_b'], tt)
    shift_new = zc3[:, t_real - 1, :C_COLS]
    return (rows(oa), rows(ob), rows(oc)), (a_rows[0], a_rows[1], a_rows[2], ret_s, wkv_s, shift_new)


def kernel(x_prompt, x_sample, c_prompt, c_sample, cache_a0, cache_a1, cache_a2, state_ret, state_wkv,
           state_shift, w_ada, b_ada, norm1, norm2, w_in, mu_c, w_decay0, w_decay_up, a0, a_up, g_up, k_k,
           k_a, r_k, lnx_w, lnx_b, w_up_a, w_up_b, w_up_c, w_out, w_router, b_router, w_exp_gu, w_exp_down,
           w_sh_gu, w_sh_down, norm_f):
    bp, tp, d = x_prompt.shape
    bs, ts, _ = x_sample.shape
    n_layers = w_ada.shape[0]
    mp, ms = bp * tp, bs * ts
    n_tok = mp + ms
    assert ms == COMB_T and mp % COMB_T == 0
    n_blocks = N_EXPERTS + (n_tok * TOP_K) // EXP_BMAX
    caches = (cache_a0, cache_a1, cache_a2)

    c_all = jnp.concatenate([c_prompt, c_sample, jnp.zeros((16 - bp - bs, d), F32)], axis=0)
    mod = ada_mod(c_all, w_ada, b_ada)

    xp = x_prompt.reshape(mp, d)
    xs = x_sample.reshape(ms, d)
    pos_p = jnp.arange(tp, dtype=jnp.int32)
    pos_s = PAST_LEN + jnp.arange(ts, dtype=jnp.int32)
    new_p = [[] for _ in range(6)]
    new_s = [[] for _ in range(6)]
    for l in range(n_layers):
        p = {'mu_c': mu_c[l], 'w_decay0': w_decay0[l], 'w_decay_up': w_decay_up[l], 'a0': a0[l], 'a_up': a_up[l],
             'g_up': g_up[l], 'k_k': k_k[l], 'k_a': k_a[l], 'r_k': r_k[l], 'lnx_w': lnx_w[l], 'lnx_b': lnx_b[l]}
        mods_p = [m.reshape(bp, 1, d) for m in jnp.split(mod[l, :bp], 6, axis=-1)]
        mods_s = [jnp.repeat(m, ts, axis=0).reshape(1, ms, d) for m in jnp.split(mod[l, bp:bp + bs], 6, axis=-1)]
        sh1p, sc1p, gt1p, sh2p, sc2p, gt2p = mods_p
        sh1s, sc1s, gt1s, sh2s, sc2s, gt2s = mods_s

        w_l = w_in[l]
        w_parts = (w_l[:, :A_COLS].astype(BF16),
                   w_l[:, A_COLS:A_COLS + B_COLS].astype(BF16),
                   jnp.pad(w_l[:, A_COLS + B_COLS:A_COLS + B_COLS + C_COLS], ((0, 0), (0, C_PAD - C_COLS))).astype(BF16),
                   w_l[:, A_COLS + B_COLS + C_COLS:].astype(BF16))
        tns = (1024, 1024, C_PAD // 3, 1024)
        hp = norm_mod(xp, norm1[l], sc1p, sh1p, 256, tp, BF16)
        hs = norm_mod(xs, norm1[l], sc1s, sh1s, ms, ms, BF16)
        z_p = [matmul(hp, w, 1024, tn) for w, tn in zip(w_parts, tns)]
        z_s = [matmul(hs, w, ms, tn) for w, tn in zip(w_parts, tns)]

        zeros_c = jnp.zeros((bp, 1, C_PAD), F32)
        o_p, st_p = _token_mixing(l, z_p, bp, tp, tp, pos_p, None,
                                  jnp.zeros((bp, H_B, DK_B, DV_B), F32), jnp.zeros((bp, H_C, HD_C, HD_C), F32),
                                  zeros_c, p)
        shift_s = jnp.pad(state_shift[l], ((0, 0), (0, C_PAD - C_COLS))).reshape(bs, 1, C_PAD)
        o_s, st_s = _token_mixing(l, z_s, bs, ts, SAMPLE_T_PAD, pos_s, caches, state_ret[l], state_wkv[l],
                                  shift_s, p)
        for lst, v in zip(new_p, st_p):
            lst.append(v)
        for lst, v in zip(new_s, st_s):
            lst.append(v)

        wa, wb, wc = w_up_a[l].astype(BF16), w_up_b[l].astype(BF16), w_up_c[l].astype(BF16)
        wo = w_out[l].astype(BF16)
        merged_p = upmerge(*o_p, wa, wb, wc, z_p[3], 1024, 512)
        merged_s = upmerge(*o_s, wa, wb, wc, z_s[3], ms, 512)
        x1p = proj_residual(merged_p, wo, xp, gt1p, 1024, 1024, tp)
        x1s = proj_residual(merged_s, wo, xs, gt1s, ms, 1024, ms)

        h2, h2_packed = norm_mod_moe(x1p, x1s, norm2[l], sc2p, sh2p, sc2s, sh2s, 256, tp)
        idx, wts, rank, counts = router(h2, w_router[l], b_router[l], 256)
        dst_slots, be, bsub, bcnt = route_tables(idx, rank, counts, n_blocks)
        ys = routed_experts(h2_packed, dst_slots, be, bsub, bcnt, w_exp_gu, w_exp_down, l)
        hid = ffn_gu(h2, w_sh_gu[l].astype(BF16), 1024, 256)
        shared = matmul(hid, w_sh_down[l].astype(BF16), 1024, 1024)
        xp = moe_combine(wts, ys, shared, x1p, gt2p, 0, tp)
        xs = moe_combine(wts, ys, shared, x1s, gt2s, mp, ms)

    y_prompt = final_norm(xp, norm_f, 256).reshape(bp, tp, d)
    y_sample = final_norm(xs, norm_f, ms).reshape(bs, ts, d)
    outs_p = [jnp.stack(z) for z in new_p]
    outs_s = [jnp.stack(z) for z in new_s]
    return (y_prompt, y_sample, *outs_p, *outs_s)
```

```python
import functools
import math

import jax
import jax.numpy as jnp
import numpy as np
from jax import lax
from jax.experimental import pallas as pl
from jax.experimental.pallas import tpu as pltpu

F32 = jnp.float32
BF16 = jnp.bfloat16

D_MODEL = 4096
PAST_LEN = 16384
A_PATTERNS = ((128, 1), (512, 4), (2048, 16))
H_A = 8
HD_A = 128
A_OUT = H_A * HD_A
ROT_A = HD_A // 4
ROPE_THETA = 500000.0
A_COLS = 3 * 3 * A_OUT
H_B = 4
DK_B = 128
DV_B = 256
B_OUT = H_B * DV_B
B_COLS = 2 * H_B * DK_B + 2 * B_OUT
RET_THETA = 10000.0
RET_CHUNK = 128
H_C = 16
HD_C = 64
C_W = H_C * HD_C
LORA_W = 64
LORA_A = 64
LORA_G = 160
C_COLS = 3 * C_W + LORA_W + LORA_A + LORA_G
C_PAD = 3456
LORA_PAD = C_PAD - 3 * C_W
WKV_GN_EPS = 64e-5
N_EXPERTS = 128
N_EXP_GROUPS = 8
TOPK_GROUPS = 4
TOP_K = 8
D_EXPERT = 768
ROUTED_SCALE = 2.5
NORM_EPS = 1e-6
SAMPLE_T_PAD = 8

VMEM_LIMIT = 56 * 1024 * 1024
WKV_L = 32
WKV_G = 4


def _cparams(sem):
    return pltpu.CompilerParams(dimension_semantics=sem, vmem_limit_bytes=VMEM_LIMIT)


def _bdot(a, b):
    return jnp.dot(a.astype(BF16), b.astype(BF16), preferred_element_type=F32)


def _bdot_t(a, b):
    return lax.dot_general(a.astype(BF16), b.astype(BF16), (((1,), (1,)), ((), ())),
                           preferred_element_type=F32)


def _split3(a):
    a1 = a.astype(BF16)
    r1 = a - a1.astype(F32)
    a2 = r1.astype(BF16)
    a3 = (r1 - a2.astype(F32)).astype(BF16)
    return a1, a2, a3


def _dot_general3(a, b, dims):
    a1, a2, a3 = _split3(a)
    b1, b2, b3 = _split3(b)
    dg = functools.partial(lax.dot_general, dimension_numbers=dims, preferred_element_type=F32)
    small = dg(a1, b3) + dg(a3, b1) + dg(a2, b2)
    mid = dg(a1, b2) + dg(a2, b1)
    return dg(a1, b1) + (mid + small)


_NN = (((1,), (0,)), ((), ()))
_NT = (((1,), (1,)), ((), ()))
_TN = (((0,), (0,)), ((), ()))


def _silu(x):
    return x * jax.nn.sigmoid(x)


def _ada_kernel(c_ref, w_ref, b_ref, o_ref):
    o_ref[0] = _bdot(_silu(c_ref[...]), w_ref[0]) + b_ref[0]


def ada_mod(c_all, w_ada, b_ada):
    n_layers, d, n = w_ada.shape
    rows = c_all.shape[0]
    tn = 512
    return pl.pallas_call(
        _ada_kernel,
        grid=(n_layers, n // tn),
        in_specs=[pl.BlockSpec((rows, d), lambda l, j: (0, 0)),
                  pl.BlockSpec((1, d, tn), lambda l, j: (l, 0, j)),
                  pl.BlockSpec((1, 1, tn), lambda l, j: (l, 0, j))],
        out_specs=pl.BlockSpec((1, rows, tn), lambda l, j: (l, 0, j)),
        out_shape=jax.ShapeDtypeStruct((n_layers, rows, n), F32),
        compiler_params=_cparams(("arbitrary", "arbitrary")),
        name="ada_mod",
    )(c_all, w_ada, b_ada.reshape(n_layers, 1, n))


def _row_mod_spec(mod, tile, rows_per_group, tn=None):
    per_group = mod.shape[1] == 1
    lead = (lambda i: (i * tile // rows_per_group, 0)) if per_group else (lambda i: (0, i))
    rows = 1 if per_group else tile
    if tn is None:
        return pl.BlockSpec((1, rows, mod.shape[2]), lambda i: (*lead(i), 0))
    return pl.BlockSpec((1, rows, tn), lambda i, j: (*lead(i), j))


def _norm_mod_kernel(x_ref, g_ref, sc_ref, sh_ref, o_ref):
    x = x_ref[...]
    y = x * lax.rsqrt(jnp.mean(x * x, axis=-1, keepdims=True) + NORM_EPS) * g_ref[...]
    o_ref[...] = (y * (1.0 + sc_ref[0]) + sh_ref[0]).astype(o_ref.dtype)


def norm_mod(x2, g, sc, sh, tile, rows_per_group, out_dtype):
    m, d = x2.shape
    spec = _row_mod_spec(sc, tile, rows_per_group)
    return pl.pallas_call(
        _norm_mod_kernel,
        grid=(m // tile,),
        in_specs=[pl.BlockSpec((tile, d), lambda i: (i, 0)), pl.BlockSpec((1, d), lambda i: (0, 0)), spec, spec],
        out_specs=pl.BlockSpec((tile, d), lambda i: (i, 0)),
        out_shape=jax.ShapeDtypeStruct((m, d), out_dtype),
        compiler_params=_cparams(("arbitrary",)),
        name="norm_mod",
    )(x2, g.reshape(1, d), sc, sh)


def _final_norm_kernel(x_ref, g_ref, o_ref):
    x = x_ref[...]
    o_ref[...] = x * lax.rsqrt(jnp.mean(x * x, axis=-1, keepdims=True) + NORM_EPS) * g_ref[...]


def final_norm(x2, g, tile):
    m, d = x2.shape
    return pl.pallas_call(
        _final_norm_kernel,
        grid=(m // tile,),
        in_specs=[pl.BlockSpec((tile, d), lambda i: (i, 0)), pl.BlockSpec((1, d), lambda i: (0, 0))],
        out_specs=pl.BlockSpec((tile, d), lambda i: (i, 0)),
        out_shape=jax.ShapeDtypeStruct((m, d), F32),
        compiler_params=_cparams(("arbitrary",)),
        name="final_norm",
    )(x2, g.reshape(1, d))


def _mm_kernel(x_ref, w_ref, o_ref):
    o_ref[...] = _bdot(x_ref[...], w_ref[...]).astype(o_ref.dtype)


def matmul(x, w, tm, tn, out_dtype=F32, row0=0, rows=None):
    k = x.shape[1]
    rows = x.shape[0] - row0 if rows is None else rows
    n = w.shape[1]
    rb0 = row0 // tm
    return pl.pallas_call(
        _mm_kernel,
        grid=(pl.cdiv(rows, tm), n // tn),
        in_specs=[pl.BlockSpec((tm, k), lambda i, j: (rb0 + i, 0)),
                  pl.BlockSpec((k, tn), lambda i, j: (0, j))],
        out_specs=pl.BlockSpec((tm, tn), lambda i, j: (i, j)),
        out_shape=jax.ShapeDtypeStruct((rows, n), out_dtype),
        compiler_params=_cparams(("arbitrary", "arbitrary")),
        name="proj",
    )(x, w)


def rope_tables_a(pos):
    half = ROT_A // 2
    inv = ROPE_THETA ** (-jnp.arange(half, dtype=F32) / half)
    ang = pos.astype(F32)[:, None] * inv[None, :]
    cos, sin = jnp.cos(ang), jnp.sin(ang)
    t = pos.shape[0]
    one = jnp.ones((t, HD_A - ROT_A), F32)
    zero_tail = jnp.zeros((t, HD_A - ROT_A), F32)
    zero_h = jnp.zeros((t, half), F32)
    c = jnp.concatenate([cos, cos, one], axis=1)
    sp = jnp.concatenate([zero_h, sin, zero_tail], axis=1)
    sm = jnp.concatenate([-sin, zero_h, zero_tail], axis=1)
    return c, sp, sm


def rope_tables_b(pos):
    half = DK_B // 2
    inv = RET_THETA ** (-jnp.arange(half, dtype=F32) / half)
    ang = pos.astype(F32)[:, None] * inv[None, :]
    cos, sin = jnp.cos(ang), jnp.sin(ang)
    return jnp.concatenate([cos, cos], axis=1), jnp.concatenate([-sin, sin], axis=1)


def _rope_a(x, c, sp, sm):
    half = ROT_A // 2
    return x * c + pltpu.roll(x, half, 1) * sp + pltpu.roll(x, HD_A - half, 1) * sm


def _attn_prompt_kernel(q0, k0, v0, q1, k1, v1, q2, k2, v2, c_ref, sp_ref, sm_ref, o_ref,
                        qr, kr, og, mg, sg):
    t = qr.shape[0]
    blk = 128
    n_iter = t // blk
    qi = lax.broadcasted_iota(jnp.int32, (blk, blk), 0)
    ki = lax.broadcasted_iota(jnp.int32, (blk, blk), 1)
    cur_ok = qi >= ki
    prev_ok = ki >= qi
    neg = jnp.float32(-jnp.inf)
    scale = HD_A ** -0.5
    for g, (qf, kf, vf) in enumerate(((q0, k0, v0), (q1, k1, v1), (q2, k2, v2))):
        dil = A_PATTERNS[g][1]
        nb = t // (dil * blk)
        c, sp, sm = c_ref[...], sp_ref[...], sm_ref[...]
        qr[...] = _rope_a(qf[0], c, sp, sm) * scale
        kr[...] = _rope_a(kf[0], c, sp, sm)

        def body(it, carry, g=g, dil=dil, nb=nb, vf=vf):
            r = it % dil
            n = it // dil
            cur = pl.ds(n * (blk * dil) + r, blk, stride=dil)
            q = qr[cur, :]
            lc = jnp.where(cur_ok, _bdot_t(q, kr[cur, :]), neg)
            m = jnp.max(lc, axis=-1, keepdims=True)
            if nb > 1:
                prev = pl.ds(jnp.maximum(n - 1, 0) * (blk * dil) + r, blk, stride=dil)
                lp = jnp.where(prev_ok & (n > 0), _bdot_t(q, kr[prev, :]), neg)
                m = jnp.maximum(m, jnp.max(lp, axis=-1, keepdims=True))
            pc = jnp.exp(lc - m)
            s = jnp.sum(pc, axis=-1, keepdims=True)
            o = _bdot(pc, vf[0, cur, :])
            if nb > 1:
                pp = jnp.exp(lp - m)
                s = s + jnp.sum(pp, axis=-1, keepdims=True)
                o = o + _bdot(pp, vf[0, prev, :])
            og[g, cur, :] = o
            mg[g, cur, :] = jnp.broadcast_to(m, (blk, HD_A))
            sg[g, cur, :] = jnp.broadcast_to(s, (blk, HD_A))
            return carry

        lax.fori_loop(0, n_iter, body, 0, unroll=2)
    m_all = jnp.maximum(jnp.maximum(mg[0], mg[1]), mg[2])
    num = jnp.zeros_like(m_all)
    den = jnp.zeros_like(m_all)
    for g in range(3):
        e = jnp.exp(mg[g] - m_all)
        num = num + e * og[g]
        den = den + e * sg[g]
    o_ref[0] = (num / den).astype(o_ref.dtype)


def attn_prompt(za, tabs):
    b, t, _ = za.shape
    assert t % (A_PATTERNS[-1][1] * 128) == 0
    col_specs = []
    for g in range(3):
        for j in range(3):
            col_specs.append(pl.BlockSpec((1, t, HD_A), lambda bi, h, cb=(g * 3 + j) * H_A: (bi, 0, cb + h)))
    tab_spec = pl.BlockSpec((t, HD_A), lambda bi, h: (0, 0))
    return pl.pallas_call(
        _attn_prompt_kernel,
        grid=(b, H_A),
        in_specs=col_specs + [tab_spec] * 3,
        out_specs=pl.BlockSpec((1, t, HD_A), lambda bi, h: (bi, 0, h)),
        out_shape=jax.ShapeDtypeStruct((b, t, A_OUT), BF16),
        scratch_shapes=[pltpu.VMEM((t, HD_A), F32), pltpu.VMEM((t, HD_A), F32),
                        pltpu.VMEM((3, t, HD_A), F32), pltpu.VMEM((3, t, HD_A), F32),
                        pltpu.VMEM((3, t, HD_A), F32)],
        compiler_params=_cparams(("arbitrary", "arbitrary")),
        name="attn_prompt",
    )(*([za] * 9), *tabs)


def _kv_rows_kernel(k_ref, v_ref, c_ref, sp_ref, sm_ref, o_ref):
    rt = k_ref.shape[1]
    c, sp, sm = c_ref[...], sp_ref[...], sm_ref[...]
    for h in range(H_A):
        cols = slice(h * HD_A, (h + 1) * HD_A)
        o_ref[0, pl.ds(h, rt, stride=2 * H_A), :] = _rope_a(k_ref[0, :, cols], c, sp, sm)
        o_ref[0, pl.ds(H_A + h, rt, stride=2 * H_A), :] = v_ref[0, :, cols]


def kv_rows(za, tabs, g, rows, rt):
    b, t, _ = za.shape
    off = (t - rows) // rt
    tab_spec = pl.BlockSpec((rt, HD_A), lambda bi, i: (off + i, 0))
    return pl.pallas_call(
        _kv_rows_kernel,
        grid=(b, rows // rt),
        in_specs=[pl.BlockSpec((1, rt, A_OUT), lambda bi, i: (bi, off + i, g * 3 + 1)),
                  pl.BlockSpec((1, rt, A_OUT), lambda bi, i: (bi, off + i, g * 3 + 2)),
                  tab_spec, tab_spec, tab_spec],
        out_specs=pl.BlockSpec((1, rt * 2 * H_A, HD_A), lambda bi, i: (bi, i, 0)),
        out_shape=jax.ShapeDtypeStruct((b, rows * 2 * H_A, HD_A), F32),
        compiler_params=_cparams(("arbitrary", "arbitrary")),
        name="kv_rows",
    )(za, za, *tabs)


def _attn_sample_kernel(q_ref, k_ref, v_ref, cache_ref, c_ref, sp_ref, sm_ref, num_ref, den_ref, m_ref,
                        onum_ref, oden_ref, om_ref, *, dil, n_back, final):
    rows = q_ref.shape[1]
    lb = cache_ref.shape[2]
    c, sp, sm = c_ref[...], sp_ref[...], sm_ref[...]
    q = _rope_a(q_ref[0], c, sp, sm) * (HD_A ** -0.5)
    kn = _rope_a(k_ref[0], c, sp, sm)
    kc = cache_ref[0, 0, :, 0, :, :].reshape(lb * H_A, HD_A)
    vc = cache_ref[0, 0, :, 1, :, :].reshape(lb * H_A, HD_A)
    neg = jnp.float32(-jnp.inf)
    span = n_back * dil

    row = lax.broadcasted_iota(jnp.int32, (rows, lb * H_A), 0)
    col = lax.broadcasted_iota(jnp.int32, (rows, lb * H_A), 1)
    dist = lb + (row // H_A) - (col // H_A)
    ok = ((col % H_A) == (row % H_A)) & ((dist % dil) == 0) & (dist <= span)
    lc = jnp.where(ok, _bdot_t(q, kc), neg)

    rown = lax.broadcasted_iota(jnp.int32, (rows, rows), 0)
    coln = lax.broadcasted_iota(jnp.int32, (rows, rows), 1)
    dn = (rown // H_A) - (coln // H_A)
    okn = ((coln % H_A) == (rown % H_A)) & (dn >= 0) & ((dn % dil) == 0) & (dn <= span)
    ln = jnp.where(okn, _bdot_t(q, kn), neg)

    m = jnp.maximum(jnp.max(lc, axis=-1, keepdims=True), jnp.max(ln, axis=-1, keepdims=True))
    pc = jnp.exp(lc - m)
    pn = jnp.exp(ln - m)
    s = jnp.sum(pc, axis=-1, keepdims=True) + jnp.sum(pn, axis=-1, keepdims=True)
    o = _bdot(pc, vc) + _bdot(pn, v_ref[0])

    m_prev = m_ref[0]
    m_new = jnp.maximum(m_prev, m)
    e_prev = jnp.exp(m_prev - m_new)
    e_cur = jnp.exp(m - m_new)
    num = num_ref[0] * e_prev + o * e_cur
    den = den_ref[0] * e_prev + s * e_cur
    onum_ref[0] = num / den if final else num
    oden_ref[0] = den
    om_ref[0] = m_new


def attn_sample(qkv, cache, layer, tabs, state, g, final):
    b, rows, _ = qkv[0].shape
    lb = cache.shape[2]
    win, dil = A_PATTERNS[g]
    row_spec = pl.BlockSpec((1, rows, HD_A), lambda bi: (bi, 0, 0))
    tab_spec = pl.BlockSpec((rows, HD_A), lambda bi: (0, 0))
    st = jax.ShapeDtypeStruct((b, rows, HD_A), F32)
    return pl.pallas_call(
        functools.partial(_attn_sample_kernel, dil=dil, n_back=win // dil, final=final),
        grid=(b,),
        in_specs=[row_spec, row_spec, row_spec,
                  pl.BlockSpec((1, 1, lb, 2, H_A, HD_A), lambda bi: (layer, bi, 0, 0, 0, 0)),
                  tab_spec, tab_spec, tab_spec, row_spec, row_spec, row_spec],
        out_specs=[row_spec, row_spec, row_spec],
        out_shape=[st, st, st],
        compiler_params=_cparams(("arbitrary",)),
        name="attn_sample",
    )(*qkv, cache, *tabs, *state)


def retention_tables(lc, l_real):
    log_gamma = np.log(1.0 - 2.0 ** (-5.0 - np.arange(H_B, dtype=np.float32))).astype(np.float32)
    i = np.arange(lc, dtype=np.float32)
    diff = i[:, None] - i[None, :]
    real = (i < l_real)
    inner = np.where((diff >= 0) & real[None, :], np.exp(np.maximum(diff, 0.0)[None] * log_gamma[:, None, None]), 0.0)
    qdec = np.exp((i + 1.0)[None, :] * log_gamma[:, None])[:, :, None]
    kdec = np.where(real[None, :], np.exp((l_real - 1.0 - i)[None, :] * log_gamma[:, None]), 0.0)[:, :, None]
    gl = np.exp(l_real * log_gamma)[:, None, None]
    f = lambda z: jnp.asarray(z, F32)
    return f(inner), f(qdec), f(kdec), f(gl)


def _retention_kernel(q_ref, k_ref, v_ref, g_ref, c_ref, s_ref, dm_ref, qd_ref, kd_ref, gl_ref, s0_ref,
                      o_ref, so_ref, *, lc):
    t = q_ref.shape[1]
    so_ref[0, 0] = s0_ref[0, 0]
    dm = dm_ref[0]
    qd = qd_ref[0]
    kd = kd_ref[0]
    gl = gl_ref[0]

    def body(ci, carry):
        rows = pl.ds(pl.multiple_of(ci * lc, lc), lc)
        c, s = c_ref[rows, :], s_ref[rows, :]
        qx, kx = q_ref[0, rows, :], k_ref[0, rows, :]
        q = qx * c + pltpu.roll(qx, DK_B // 2, 1) * s
        k = (kx * c + pltpu.roll(kx, DK_B // 2, 1) * s) * (DK_B ** -0.5)
        v = v_ref[0, rows, :]
        st = so_ref[0, 0]
        o = _bdot(_bdot_t(q, k) * dm, v) + _bdot(q, st) * qd
        so_ref[0, 0] = st * gl + lax.dot_general((k * kd).astype(BF16), v.astype(BF16), _TN,
                                                 preferred_element_type=F32)
        o = o * lax.rsqrt(jnp.mean(o * o, axis=-1, keepdims=True) + NORM_EPS)
        o_ref[0, rows, :] = (_silu(g_ref[0, rows, :]) * o).astype(o_ref.dtype)
        return carry

    lax.fori_loop(0, t // lc, body, 0)


def retention(zb, tabs, dec, s0, lc):
    b, t, _ = zb.shape
    nqk = H_B * DK_B // DV_B
    tab_spec = pl.BlockSpec((t, DK_B), lambda bi, h: (0, 0))
    return pl.pallas_call(
        functools.partial(_retention_kernel, lc=lc),
        grid=(b, H_B),
        in_specs=[pl.BlockSpec((1, t, DK_B), lambda bi, h: (bi, 0, h)),
                  pl.BlockSpec((1, t, DK_B), lambda bi, h: (bi, 0, H_B + h)),
                  pl.BlockSpec((1, t, DV_B), lambda bi, h: (bi, 0, 2 * nqk + h)),
                  pl.BlockSpec((1, t, DV_B), lambda bi, h: (bi, 0, 2 * nqk + H_B + h)),
                  tab_spec, tab_spec,
                  pl.BlockSpec((1, lc, lc), lambda bi, h: (h, 0, 0)),
                  pl.BlockSpec((1, lc, 1), lambda bi, h: (h, 0, 0)),
                  pl.BlockSpec((1, lc, 1), lambda bi, h: (h, 0, 0)),
                  pl.BlockSpec((1, 1, 1), lambda bi, h: (h, 0, 0)),
                  pl.BlockSpec((1, 1, DK_B, DV_B), lambda bi, h: (bi, h, 0, 0))],
        out_specs=[pl.BlockSpec((1, t, DV_B), lambda bi, h: (bi, 0, h)),
                   pl.BlockSpec((1, 1, DK_B, DV_B), lambda bi, h: (bi, h, 0, 0))],
        out_shape=[jax.ShapeDtypeStruct((b, t, B_OUT), BF16),
                   jax.ShapeDtypeStruct((b, H_B, DK_B, DV_B), F32)],
        compiler_params=_cparams(("arbitrary", "arbitrary")),
        name="retention",
    )(zb, zb, zb, zb, *tabs, *dec, s0)


def _seg_sum(x, seg_ref):
    x1 = x.astype(BF16)
    x2 = (x - x1.astype(F32)).astype(BF16)
    seg = seg_ref[...]
    return jnp.dot(x1, seg, preferred_element_type=F32) + jnp.dot(x2, seg, preferred_element_type=F32)


def _wkv_prep_kernel(zc_ref, zp_ref, sh_ref, mu_ref, wl_ref, w0_ref, a0_ref, kk_ref, ka_ref, rk_ref,
                     seg_ref, tri_ref, ones_ref,
                     rt_ref, at_ref, bt_ref, kt_ref, bh_ref, kh_ref, v_ref, bon_ref, g_ref, wlast_ref,
                     *, t_valid):
    i = pl.program_id(1)
    tt = zc_ref.shape[1]
    zc = zc_ref[0]
    row = lax.broadcasted_iota(jnp.int32, (tt, 1), 0)
    first = jnp.where(i == 0, sh_ref[0], zp_ref[0, 7:8, :])
    prev = jnp.where(row == 0, first, pltpu.roll(zc, 1, 0))
    xc = zc + (prev - zc) * mu_ref[...]
    r = xc[:, 0:C_W]
    k = xc[:, C_W:2 * C_W]
    v = xc[:, 2 * C_W:3 * C_W]
    xl = xc[:, 3 * C_W:]
    lane = lax.broadcasted_iota(jnp.int32, xl.shape, 1)
    act = jnp.where(lane < LORA_W, jnp.tanh(xl),
                    jnp.where(lane < LORA_W + LORA_A, xl,
                              jnp.where(lane < LORA_W + LORA_A + LORA_G, jax.nn.sigmoid(xl), 0.0)))
    lora = _bdot(act, wl_ref[...])
    wl = w0_ref[...] + lora[:, 0:C_W]
    softplus = jnp.maximum(-wl, 0.0) + jnp.log(1.0 + jnp.exp(-jnp.abs(wl)))
    lw = -jnp.exp(-softplus - 0.5)
    a = jax.nn.sigmoid(a0_ref[...] + lora[:, C_W:2 * C_W])
    g_ref[0] = lora[:, 2 * C_W:3 * C_W]
    kk = k * kk_ref[...]
    kk = kk / jnp.maximum(jnp.sqrt(_seg_sum(kk * kk, seg_ref)), 1e-12)
    k2 = k * (1.0 + (a - 1.0) * ka_ref[...])
    bon_ref[0] = _seg_sum(r * k2 * rk_ref[...], seg_ref) * v
    al = -kk
    be = kk * a
    if t_valid is not None:
        ok = (i * tt + row) < t_valid
        lw = jnp.where(ok, lw, 0.0)
        al = jnp.where(ok, al, 0.0)
        be = jnp.where(ok, be, 0.0)
        k2 = jnp.where(ok, k2, 0.0)
    l1, l2, l3 = _split3(lw)
    tri = tri_ref[...]
    ones = ones_ref[...]
    dot = functools.partial(jnp.dot, preferred_element_type=F32)
    cum = dot(tri, l1) + (dot(tri, l2) + dot(tri, l3))
    tot = dot(ones, l1) + (dot(ones, l2) + dot(ones, l3))
    winv = jnp.exp(-cum)
    wrest = jnp.exp(tot - cum)
    rt_ref[0] = r * jnp.exp(cum)
    at_ref[0] = al * jnp.exp(cum - lw)
    bt_ref[0] = be * winv
    kt_ref[0] = k2 * winv
    bh_ref[0] = be * wrest
    kh_ref[0] = k2 * wrest
    v_ref[0] = v
    wl_all = jnp.exp(tot)
    for c in range(tt // WKV_L):
        wlast_ref[0, c:c + 1, :] = wl_all[c * WKV_L:c * WKV_L + 1, :]


def _blockdiag_ones(n, blk, lower):
    i = np.arange(n)
    same = (i[:, None] // blk) == (i[None, :] // blk)
    m = same & (i[:, None] >= i[None, :]) if lower else same
    return jnp.asarray(m, BF16)


def wkv_prep(zc, shift_row, p, tt, t_valid):
    b, t, _ = zc.shape
    row = lambda z: z.reshape(1, -1)
    mu = jnp.pad(p['mu_c'], (0, C_PAD - C_COLS)).reshape(1, C_PAD)
    w_lora = jnp.zeros((LORA_PAD, 3 * C_W), F32)
    w_lora = w_lora.at[0:LORA_W, 0:C_W].set(p['w_decay_up'])
    w_lora = w_lora.at[LORA_W:LORA_W + LORA_A, C_W:2 * C_W].set(p['a_up'])
    w_lora = w_lora.at[LORA_W + LORA_A:LORA_W + LORA_A + LORA_G, 2 * C_W:].set(p['g_up']).astype(BF16)
    seg = _blockdiag_ones(C_W, HD_C, False)
    tri = _blockdiag_ones(tt, WKV_L, True)
    ones = _blockdiag_ones(tt, WKV_L, False)
    vec = pl.BlockSpec((1, C_W), lambda bi, i: (0, 0))
    const = lambda shape: pl.BlockSpec(shape, lambda bi, i: (0, 0))
    blk = pl.BlockSpec((1, tt, C_W), lambda bi, i: (bi, i, 0))
    full = jax.ShapeDtypeStruct((b, t, C_W), F32)
    return pl.pallas_call(
        functools.partial(_wkv_prep_kernel, t_valid=t_valid),
        grid=(b, t // tt),
        in_specs=[pl.BlockSpec((1, tt, C_PAD), lambda bi, i: (bi, i, 0)),
                  pl.BlockSpec((1, 8, C_PAD), lambda bi, i: (bi, jnp.maximum(i * (tt // 8) - 1, 0), 0)),
                  pl.BlockSpec((1, 1, C_PAD), lambda bi, i: (bi, 0, 0)),
                  const((1, C_PAD)), const((LORA_PAD, 3 * C_W)), vec, vec, vec, vec, vec,
                  const((C_W, C_W)), const((tt, tt)), const((tt, tt))],
        out_specs=[blk] * 9 + [pl.BlockSpec((1, tt // WKV_L, C_W), lambda bi, i: (bi, i, 0))],
        out_shape=[full] * 9 + [jax.ShapeDtypeStruct((b, t // WKV_L, C_W), F32)],
        compiler_params=_cparams(("arbitrary", "arbitrary")),
        name="wkv_prep",
    )(zc, zc, shift_row, mu, w_lora, row(p['w_decay0']), row(p['a0']), row(p['k_k']), row(p['k_a']),
      row(p['r_k']), seg, tri, ones)


def _wkv_chunk_kernel(rt_ref, at_ref, bt_ref, kt_ref, bh_ref, kh_ref, v_ref, wl_ref, s0_ref,
                      mst_ref, pm_ref, eye_ref, y_ref, so_ref):
    tt = rt_ref.shape[1]
    n = WKV_G * WKV_L
    w = WKV_G * HD_C

    @pl.when(pl.program_id(1) == 0)
    def _init():
        so_ref[...] = s0_ref[...]

    mst = mst_ref[...]
    pm = pm_ref[...]
    eye = eye_ref[...]

    def stack(x):
        return (jnp.concatenate([x] * WKV_G, axis=0) * mst).astype(BF16)

    def body(ci, carry):
        rows = pl.ds(pl.multiple_of(ci * WKV_L, WKV_L), WKV_L)
        groups = range(H_C // WKV_G)
        lanes = [slice(hg * w, (hg + 1) * w) for hg in groups]
        ld = lambda ref, hg: stack(ref[0, rows, lanes[hg]])
        lhs = [jnp.concatenate([ld(at_ref, hg), ld(rt_ref, hg)], axis=0) for hg in groups]
        rhs = [jnp.concatenate([ld(bt_ref, hg), ld(kt_ref, hg)], axis=0) for hg in groups]
        vst = [ld(v_ref, hg) for hg in groups]
        s_bd = [so_ref[0, hg] for hg in groups]
        p = [_bdot_t(lhs[hg], rhs[hg]) * pm for hg in groups]
        q = [_bdot_t(lhs[hg], s_bd[hg]) for hg in groups]
        nil = [p[hg][:n, :n] for hg in groups]
        x = [eye + nil[hg] for hg in groups]
        for _ in range(int(math.log2(WKV_L)) - 1):
            nil = [_bdot(nil[hg], nil[hg]) for hg in groups]
            x = [x[hg] + _bdot(x[hg], nil[hg]) for hg in groups]
        rhs_u = [q[hg][:n] + _bdot(p[hg][:n, n:], vst[hg]) for hg in groups]
        u = [_bdot(x[hg], rhs_u[hg]) for hg in groups]
        uv = [jnp.concatenate([u[hg].astype(BF16), vst[hg]], axis=0) for hg in groups]
        y = [q[hg][n:] + _bdot(p[hg][n:], uv[hg]) for hg in groups]
        for hg in groups:
            acc = y[hg][0:WKV_L]
            for g in range(1, WKV_G):
                acc = acc + y[hg][g * WKV_L:(g + 1) * WKV_L]
            y_ref[0, rows, lanes[hg]] = acc
        hat = [jnp.concatenate([ld(bh_ref, hg), ld(kh_ref, hg)], axis=0) for hg in groups]
        upd = [lax.dot_general(uv[hg], hat[hg], _TN, preferred_element_type=F32) for hg in groups]
        for hg in groups:
            so_ref[0, hg] = s_bd[hg] * wl_ref[0, pl.ds(ci, 1), lanes[hg]] + upd[hg]
        return carry

    lax.fori_loop(0, tt // WKV_L, body, 0)


def wkv_chunk(prep, s0):
    rt, at, bt, kt, bh, kh, v, _, _, wlast = prep
    b, t, _ = rt.shape
    ng = H_C // WKV_G
    w = WKV_G * HD_C
    n = WKV_G * WKV_L
    s0g = s0.astype(F32).reshape(b, ng, WKV_G, HD_C, HD_C)
    eye_g = jnp.eye(WKV_G, dtype=F32)
    s_bd = jnp.einsum('bngij,gh->bngihj', s0g, eye_g).reshape(b, ng, w, w)
    mst = jnp.asarray((np.arange(n)[:, None] // WKV_L) == (np.arange(w)[None, :] // HD_C), F32)
    pi = np.arange(2 * n)
    pm_np = np.where((pi[:, None] < n), (pi[:, None] % n) > (pi[None, :] % n), (pi[:, None] % n) >= (pi[None, :] % n))
    pm = jnp.asarray(pm_np, F32)
    eye = jnp.eye(n, dtype=F32)
    tt = min(t, 512)
    blk = pl.BlockSpec((1, tt, C_W), lambda bi, ti: (bi, ti, 0))
    const = lambda shape: pl.BlockSpec(shape, lambda bi, ti: (0, 0))
    st_spec = pl.BlockSpec((1, ng, w, w), lambda bi, ti: (bi, 0, 0, 0))
    y, s_new = pl.pallas_call(
        _wkv_chunk_kernel,
        grid=(b, t // tt),
        in_specs=[blk] * 7 + [pl.BlockSpec((1, tt // WKV_L, C_W), lambda bi, ti: (bi, ti, 0)), st_spec,
                              const((n, w)), const((2 * n, 2 * n)), const((n, n))],
        out_specs=[blk, st_spec],
        out_shape=[jax.ShapeDtypeStruct((b, t, C_W), F32), jax.ShapeDtypeStruct((b, ng, w, w), F32)],
        compiler_params=_cparams(("arbitrary", "arbitrary")),
        name="wkv_chunk",
    )(rt, at, bt, kt, bh, kh, v, wlast, s_bd, mst, pm, eye)
    s_new = jnp.einsum('bngihj,gh->bngij', s_new.reshape(b, ng, WKV_G, HD_C, WKV_G, HD_C), eye_g)
    return y, s_new.reshape(b, H_C, HD_C, HD_C)


def _wkv_post_kernel(y_ref, bon_ref, g_ref, lw_ref, lb_ref, seg_ref, o_ref):
    y = y_ref[0]
    mean = _seg_sum(y, seg_ref) * (1.0 / HD_C)
    d = y - mean
    var = _seg_sum(d * d, seg_ref) * (1.0 / HD_C)
    yn = d * lax.rsqrt(var + WKV_GN_EPS) * lw_ref[...] + lb_ref[...]
    o_ref[0] = ((yn + bon_ref[0]) * g_ref[0]).astype(o_ref.dtype)


def wkv_post(y, bon, g, lnx_w, lnx_b, tt):
    b, t, _ = y.shape
    blk = pl.BlockSpec((1, tt, C_W), lambda bi, i: (bi, i, 0))
    vec = pl.BlockSpec((1, C_W), lambda bi, i: (0, 0))
    return pl.pallas_call(
        _wkv_post_kernel,
        grid=(b, t // tt),
        in_specs=[blk, blk, blk, vec, vec, pl.BlockSpec((C_W, C_W), lambda bi, i: (0, 0))],
        out_specs=blk,
        out_shape=jax.ShapeDtypeStruct((b, t, C_W), BF16),
        compiler_params=_cparams(("arbitrary", "arbitrary")),
        name="wkv_post",
    )(y, bon, g, lnx_w.reshape(1, C_W), lnx_b.reshape(1, C_W), _blockdiag_ones(C_W, HD_C, False))


def _upmerge_kernel(oa_ref, ob_ref, oc_ref, wa_ref, wb_ref, wc_ref, g0_ref, g1_ref, g2_ref, o_ref):
    acc = jax.nn.sigmoid(g0_ref[...]) * jnp.dot(oa_ref[...], wa_ref[...], preferred_element_type=F32)
    acc = acc + jax.nn.sigmoid(g1_ref[...]) * jnp.dot(ob_ref[...], wb_ref[...], preferred_element_type=F32)
    acc = acc + jax.nn.sigmoid(g2_ref[...]) * jnp.dot(oc_ref[...], wc_ref[...], preferred_element_type=F32)
    o_ref[...] = acc.astype(o_ref.dtype)


def upmerge(oa, ob, oc, wa, wb, wc, zg, tm, tn):
    m = oa.shape[0]
    d = wa.shape[1]
    nj = d // tn
    act = pl.BlockSpec((tm, oa.shape[1]), lambda i, j: (i, 0))
    wsp = pl.BlockSpec((wa.shape[0], tn), lambda i, j: (0, j))
    gate = lambda br: pl.BlockSpec((tm, tn), lambda i, j: (i, br * nj + j))
    return pl.pallas_call(
        _upmerge_kernel,
        grid=(m // tm, nj),
        in_specs=[act, act, act, wsp, wsp, wsp, gate(0), gate(1), gate(2)],
        out_specs=pl.BlockSpec((tm, tn), lambda i, j: (i, j)),
        out_shape=jax.ShapeDtypeStruct((m, d), BF16),
        compiler_params=_cparams(("arbitrary", "arbitrary")),
        name="upmerge",
    )(oa, ob, oc, wa, wb, wc, zg, zg, zg)


def _proj_res_kernel(a_ref, w_ref, x_ref, gt_ref, o_ref):
    o_ref[...] = x_ref[...] + gt_ref[0] * _bdot(a_ref[...], w_ref[...])


def proj_residual(a, w, x, gt, tm, tn, rows_per_group):
    m, k = a.shape
    n = w.shape[1]
    return pl.pallas_call(
        _proj_res_kernel,
        grid=(m // tm, n // tn),
        in_specs=[pl.BlockSpec((tm, k), lambda i, j: (i, 0)),
                  pl.BlockSpec((k, tn), lambda i, j: (0, j)),
                  pl.BlockSpec((tm, tn), lambda i, j: (i, j)),
                  _row_mod_spec(gt, tm, rows_per_group, tn)],
        out_specs=pl.BlockSpec((tm, tn), lambda i, j: (i, j)),
        out_shape=jax.ShapeDtypeStruct((m, n), F32),
        compiler_params=_cparams(("arbitrary", "arbitrary")),
        name="proj_residual",
    )(a, w, x, gt)


EXP_SUB = 128
EXP_NSUB = 5
EXP_BMAX = EXP_SUB * EXP_NSUB
EXP_CE = 256
EXP_NC = D_EXPERT // EXP_CE
EXP_KSLABS = 8
EXP_JSLABS = 4
EXP_VMEM_LIMIT = 62 * 1024 * 1024
COMB_T = 32
COMB_ROWS = COMB_T * TOP_K


def _router_kernel(h_ref, w_ref, b_ref, tri_ref, idx_ref, wts_ref, rank_ref, cnt_ref, carry, *, n_valid):
    h1, h2, _ = _split3(h_ref[...])
    w1, w2, _ = _split3(w_ref[...])
    dot = functools.partial(jnp.dot, preferred_element_type=F32)
    logits = dot(h1, w1) + (dot(h1, w2) + dot(h2, w1))
    scores = jax.nn.sigmoid(logits)
    choice = scores + b_ref[...]
    tm = choice.shape[0]
    lane = lax.broadcasted_iota(jnp.int32, (tm, N_EXPERTS), 1)
    per_group = N_EXPERTS // N_EXP_GROUPS
    grp = lane // per_group
    neg = jnp.float32(-jnp.inf)

    def first_argmax(x, ids, sentinel):
        m = jnp.max(x, axis=-1, keepdims=True)
        return m, jnp.min(jnp.where(x == m, ids, sentinel), axis=-1, keepdims=True)

    gscore = jnp.zeros_like(choice)
    for g in range(N_EXP_GROUPS):
        xg = jnp.where(grp == g, choice, neg)
        m1, i1 = first_argmax(xg, lane, N_EXPERTS)
        m2 = jnp.max(jnp.where(lane == i1, neg, xg), axis=-1, keepdims=True)
        gscore = jnp.where(grp == g, m1 + m2, gscore)
    keep = jnp.zeros(choice.shape, jnp.bool_)
    for _ in range(TOPK_GROUPS):
        _, gi = first_argmax(gscore, grp, N_EXP_GROUPS)
        keep = keep | (grp == gi)
        gscore = jnp.where(grp == gi, neg, gscore)
    masked = jnp.where(keep, choice, neg)
    lane_k = lax.broadcasted_iota(jnp.int32, (tm, TOP_K), 1)
    idx = jnp.zeros((tm, TOP_K), jnp.int32)
    wts = jnp.zeros((tm, TOP_K), F32)
    hits = []
    for k in range(TOP_K):
        _, ik = first_argmax(masked, lane, N_EXPERTS)
        hit = lane == ik
        hits.append(hit)
        wk = jnp.sum(jnp.where(hit, scores, 0.0), axis=-1, keepdims=True)
        masked = jnp.where(hit, neg, masked)
        idx = jnp.where(lane_k == k, ik, idx)
        wts = jnp.where(lane_k == k, wk, wts)
    wts_ref[...] = wts / jnp.sum(wts, axis=-1, keepdims=True) * ROUTED_SCALE
    idx_ref[...] = idx

    @pl.when(pl.program_id(0) == 0)
    def _init():
        carry[...] = jnp.zeros_like(carry)

    row = pl.program_id(0) * tm + lax.broadcasted_iota(jnp.int32, (tm, 1), 0)
    sel = jnp.zeros(choice.shape, F32)
    for hit in hits:
        sel = sel + jnp.where(hit, 1.0, 0.0)
    sel = jnp.where(row < n_valid, sel, 0.0)
    before = jnp.dot(tri_ref[...], sel.astype(BF16), preferred_element_type=F32) + carry[...]
    rank = jnp.zeros((tm, TOP_K), jnp.int32)
    for k, hit in enumerate(hits):
        rk = jnp.sum(jnp.where(hit, before, 0.0), axis=-1, keepdims=True)
        rank = jnp.where(lane_k == k, rk.astype(jnp.int32), rank)
    rank_ref[...] = rank
    carry[...] = carry[...] + jnp.sum(sel, axis=0, keepdims=True)
    cnt_ref[...] = carry[...].astype(jnp.int32)


def router(h, w_router, b_router, tm):
    n, d = h.shape
    tri = jnp.asarray(np.arange(tm)[:, None] > np.arange(tm)[None, :], BF16)
    tok = pl.BlockSpec((tm, TOP_K), lambda i: (i, 0))
    return pl.pallas_call(
        functools.partial(_router_kernel, n_valid=n),
        grid=(pl.cdiv(n, tm),),
        in_specs=[pl.BlockSpec((tm, d), lambda i: (i, 0)),
                  pl.BlockSpec((d, N_EXPERTS), lambda i: (0, 0)),
                  pl.BlockSpec((1, N_EXPERTS), lambda i: (0, 0)),
                  pl.BlockSpec((tm, tm), lambda i: (0, 0))],
        out_specs=[tok, tok, tok, pl.BlockSpec((1, N_EXPERTS), lambda i: (0, 0))],
        out_shape=[jax.ShapeDtypeStruct((n, TOP_K), jnp.int32), jax.ShapeDtypeStruct((n, TOP_K), F32),
                   jax.ShapeDtypeStruct((n, TOP_K), jnp.int32), jax.ShapeDtypeStruct((1, N_EXPERTS), jnp.int32)],
        scratch_shapes=[pltpu.VMEM((1, N_EXPERTS), F32)],
        compiler_params=_cparams(("arbitrary",)),
        name="router",
    )(h, w_router, b_router.reshape(1, N_EXPERTS), tri)


def _pack_bf16_pair(y):
    half = y.shape[1] // 2
    bits = lax.bitcast_convert_type(y.astype(BF16).astype(F32), jnp.uint32)
    return (bits[:, :half] >> 16) | (bits[:, half:] & jnp.uint32(0xFFFF0000))


def _unpack_bf16_pair(u):
    lo = lax.bitcast_convert_type(u << 16, F32).astype(BF16)
    hi = lax.bitcast_convert_type(u & jnp.uint32(0xFFFF0000), F32).astype(BF16)
    return lo, hi


def _norm_mod_moe_kernel(xp_ref, xs_ref, g_ref, scp_ref, shp_ref, scs_ref, shs_ref, h_ref, hp_ref, *, n_tiles):
    i = pl.program_id(0)

    def emit(x, sc, sh):
        rows = x.shape[0]
        y = x * lax.rsqrt(jnp.mean(x * x, axis=-1, keepdims=True) + NORM_EPS) * g_ref[...]
        y = y * (1.0 + sc) + sh
        h_ref[0:rows, :] = y
        hp_ref[0:rows, :] = _pack_bf16_pair(y)

    @pl.when(i < n_tiles)
    def _prompt():
        emit(xp_ref[...], scp_ref[0], shp_ref[0])

    @pl.when(i == n_tiles)
    def _sample():
        emit(xs_ref[...], scs_ref[0], shs_ref[0])


def norm_mod_moe(xp, xs, g, scp, shp, scs, shs, tile, rows_per_group):
    mp, d = xp.shape
    ms = xs.shape[0]
    n_tiles = mp // tile
    clamp = lambda i: jnp.minimum(i, n_tiles - 1)
    pspec = pl.BlockSpec((1, 1, d), lambda i: (clamp(i) * tile // rows_per_group, 0, 0))
    sspec = pl.BlockSpec((1, ms, d), lambda i: (0, 0, 0))
    return pl.pallas_call(
        functools.partial(_norm_mod_moe_kernel, n_tiles=n_tiles),
        grid=(n_tiles + 1,),
        in_specs=[pl.BlockSpec((tile, d), lambda i: (clamp(i), 0)),
                  pl.BlockSpec((ms, d), lambda i: (0, 0)),
                  pl.BlockSpec((1, d), lambda i: (0, 0)), pspec, pspec, sspec, sspec],
        out_specs=[pl.BlockSpec((tile, d), lambda i: (i, 0)), pl.BlockSpec((tile, d // 2), lambda i: (i, 0))],
        out_shape=[jax.ShapeDtypeStruct((mp + ms, d), F32), jax.ShapeDtypeStruct((mp + ms, d // 2), jnp.uint32)],
        compiler_params=_cparams(("arbitrary",)),
        name="norm_mod_moe",
    )(xp, xs, g.reshape(1, d), scp, shp, scs, shs)


def route_tables(idx, rank, counts, n_blocks):
    n = idx.shape[0]
    counts = counts.reshape(N_EXPERTS)
    padded = (counts + EXP_SUB - 1) // EXP_SUB * EXP_SUB
    pad_end = jnp.cumsum(padded)
    pad_start = pad_end - padded
    onehot = idx[:, :, None] == jnp.arange(N_EXPERTS, dtype=jnp.int32)[None, None, :]
    slot_of = rank + jnp.sum(jnp.where(onehot, pad_start[None, None, :], 0), axis=-1)
    tok = jnp.arange(n, dtype=jnp.int32)[:, None]
    kk = jnp.arange(TOP_K, dtype=jnp.int32)[None, :]
    dst_of = (tok // COMB_T) * COMB_ROWS + kk * COMB_T + tok % COMB_T
    n_sub = (n * TOP_K) // EXP_SUB + N_EXPERTS + EXP_NSUB
    dst_slots = jnp.zeros((n_sub * EXP_SUB,), jnp.int32).at[slot_of.reshape(-1)].set(dst_of.reshape(-1))
    nblk = (counts + EXP_BMAX - 1) // EXP_BMAX
    blk_end = jnp.cumsum(nblk)
    bid = jnp.arange(n_blocks, dtype=jnp.int32)
    used = bid < blk_end[-1]
    be = jnp.sum((bid[:, None] >= blk_end[None, :]).astype(jnp.int32), axis=1)
    be = jnp.minimum(be, N_EXPERTS - 1)
    sel = be[:, None] == jnp.arange(N_EXPERTS, dtype=jnp.int32)[None, :]
    pick = lambda v: jnp.sum(jnp.where(sel, v[None, :], 0), axis=1)
    within = bid - pick(blk_end - nblk)
    bcnt = jnp.where(used, jnp.clip(pick(counts) - within * EXP_BMAX, 0, EXP_BMAX), 0).astype(jnp.int32)
    bsub = jnp.where(used, (pick(pad_start) + within * EXP_BMAX) // EXP_SUB, 0).astype(jnp.int32)
    last_e = jnp.sum(jnp.where(bid == blk_end[-1] - 1, be, 0))
    be = jnp.where(used, be, last_e).astype(jnp.int32)
    return dst_slots.reshape(n_sub, EXP_SUB), be, bsub, bcnt


ROW_UNROLL = 8


def _for_rows(n_rows, fn):
    n_main = n_rows // ROW_UNROLL

    def group(i, carry):
        base = pl.multiple_of(i * ROW_UNROLL, ROW_UNROLL)
        for j in range(ROW_UNROLL):
            fn(base + j)
        return carry

    def single(r, carry):
        fn(r)
        return carry

    lax.fori_loop(0, n_main, group, 0)
    lax.fori_loop(n_main * ROW_UNROLL, n_rows, single, 0)


def _expert_kernel(be_ref, bsub_ref, bcnt_ref, dst_hbm, hp_hbm, wg_ref, wu_ref, wd_ref, ys_hbm,
                   dst_smem, xu, xb, wgu_b, wd_b, acc, ost, sem_dst, sem_in, sem_out):
    vb = pl.program_id(0)
    c = pl.program_id(1)
    nb = pl.num_programs(0)
    cnt = bcnt_ref[vb]
    prev = jnp.maximum(vb - 1, 0)
    prev_cnt = jnp.where(vb > 0, bcnt_ref[prev], 0)
    nxt = jnp.minimum(vb + 1, nb - 1)
    next_cnt = jnp.where(vb + 1 < nb, bcnt_ref[nxt], 0)
    nxt2 = jnp.minimum(vb + 2, nb - 1)
    next2_cnt = jnp.where(vb + 2 < nb, bcnt_ref[nxt2], 0)
    half = xu.shape[1]
    n_tab = dst_smem.shape[0]
    sub_bits = EXP_SUB.bit_length() - 1
    t_bits = COMB_T.bit_length() - 1
    rows_bits = COMB_ROWS.bit_length() - 1

    def dst_copy(block):
        return pltpu.make_async_copy(dst_hbm.at[pl.ds(bsub_ref[block], EXP_NSUB + 1)],
                                     dst_smem.at[block % n_tab], sem_dst)

    def dst_at(block, r):
        return dst_smem[block % n_tab, r >> sub_bits, r & (EXP_SUB - 1)]

    def gather_copy(block, r):
        d = dst_at(block, r)
        tok = ((d >> rows_bits) << t_bits) | (d & (COMB_T - 1))
        return pltpu.make_async_copy(hp_hbm.at[pl.ds(tok, 1)], xu.at[pl.ds(r, 1)], sem_in)

    def scatter_copy(block, r):
        return pltpu.make_async_copy(ost.at[pl.ds(r, 1)], ys_hbm.at[pl.ds(dst_at(block, r), 1)], sem_out)

    def gather_wait(n_rows):
        _for_rows(n_rows, lambda r: pltpu.make_async_copy(
            hp_hbm.at[pl.ds(0, 1)], xu.at[pl.ds(0, 1)], sem_in).wait())

    def scatter_wait(n_rows):
        _for_rows(n_rows, lambda r: pltpu.make_async_copy(
            ost.at[pl.ds(0, 1)], ys_hbm.at[pl.ds(0, 1)], sem_out).wait())

    @pl.when((vb == 0) & (c == 0))
    def _prologue():
        xu[...] = jnp.zeros_like(xu)
        for blk in (0, 1):
            cp = dst_copy(jnp.minimum(blk, nb - 1))
            cp.start()
            cp.wait()
        _for_rows(cnt, lambda r: gather_copy(0, r).start())

    @pl.when((cnt == 0) & (prev_cnt > 0) & (c == 0))
    def _drain():
        _for_rows(prev_cnt, lambda r: scatter_copy(prev, r).start())
        scatter_wait(prev_cnt)

    @pl.when(cnt > 0)
    def _block():
        @pl.when(c == 0)
        def _arrive():
            gather_wait(cnt)
            lo, hi = _unpack_bf16_pair(xu[...])
            xb[:, :half] = lo
            xb[:, half:] = hi

            @pl.when((vb > 0) & (next_cnt > 0))
            def _():
                dst_copy(nxt).wait()

            _for_rows(next_cnt, lambda r: gather_copy(nxt, r).start())
            _for_rows(prev_cnt, lambda r: scatter_copy(prev, r).start())

        n_k = wgu_b.shape[0]
        kw = wgu_b.shape[1]
        n_j = wd_b.shape[0]
        jw = wd_b.shape[2]

        def cast_gu(kh):
            rows = slice(kh * kw, (kh + 1) * kw)
            wgu_b[kh, :, :EXP_CE] = wg_ref[0, 0, rows, :].astype(BF16)
            wgu_b[kh, :, EXP_CE:] = wu_ref[0, 0, rows, :].astype(BF16)

        def cast_down(j):
            wd_b[j] = wd_ref[0, 0, :, j * jw:(j + 1) * jw].astype(BF16)

        cast_gu(0)
        hid = None
        for kh in range(n_k):
            if kh + 1 < n_k:
                cast_gu(kh + 1)
            else:
                cast_down(0)
            part = jnp.dot(xb[:, kh * kw:(kh + 1) * kw], wgu_b[kh], preferred_element_type=F32)
            hid = part if hid is None else hid + part
        act = (_silu(hid[:, :EXP_CE]) * hid[:, EXP_CE:]).astype(BF16)
        first = c == 0
        for j in range(n_j):
            if j + 1 < n_j:
                cast_down(j + 1)
            cols = slice(j * jw, (j + 1) * jw)
            yj = jnp.dot(act, wd_b[j], preferred_element_type=F32)
            acc[:, cols] = jnp.where(first, yj, acc[:, cols] + yj)

        @pl.when(c == EXP_NC - 1)
        def _finish():
            scatter_wait(prev_cnt)
            ost[...] = _pack_bf16_pair(acc[...])

            @pl.when(next2_cnt > 0)
            def _():
                dst_copy(nxt2).start()


def routed_experts(hp, dst_slots, be, bsub, bcnt, w_gu, w_down, layer):
    n, half = hp.shape
    d = 2 * half
    n_blocks = be.shape[0]

    def chunk(vb, c, bc_r):
        return jnp.where(bc_r[vb] > 0, c, EXP_NC - 1)

    grid_spec = pltpu.PrefetchScalarGridSpec(
        num_scalar_prefetch=3,
        grid=(n_blocks, EXP_NC),
        in_specs=[pl.BlockSpec(memory_space=pl.ANY),
                  pl.BlockSpec(memory_space=pl.ANY),
                  pl.BlockSpec((1, 1, d, EXP_CE), lambda vb, c, be_r, bs_r, bc_r: (layer, be_r[vb], 0, chunk(vb, c, bc_r))),
                  pl.BlockSpec((1, 1, d, EXP_CE), lambda vb, c, be_r, bs_r, bc_r: (layer, be_r[vb], 0, EXP_NC + chunk(vb, c, bc_r))),
                  pl.BlockSpec((1, 1, EXP_CE, d), lambda vb, c, be_r, bs_r, bc_r: (layer, be_r[vb], chunk(vb, c, bc_r), 0))],
        out_specs=pl.BlockSpec(memory_space=pl.ANY),
        scratch_shapes=[pltpu.SMEM((3, EXP_NSUB + 1, EXP_SUB), jnp.int32),
                        pltpu.VMEM((EXP_BMAX, half), jnp.uint32),
                        pltpu.VMEM((EXP_BMAX, d), BF16),
                        pltpu.VMEM((EXP_KSLABS, d // EXP_KSLABS, 2 * EXP_CE), BF16),
                        pltpu.VMEM((EXP_JSLABS, EXP_CE, d // EXP_JSLABS), BF16),
                        pltpu.VMEM((EXP_BMAX, d), F32),
                        pltpu.VMEM((EXP_BMAX, half), jnp.uint32),
                        pltpu.SemaphoreType.DMA(()), pltpu.SemaphoreType.DMA(()), pltpu.SemaphoreType.DMA(())])
    return pl.pallas_call(
        _expert_kernel,
        grid_spec=grid_spec,
        out_shape=jax.ShapeDtypeStruct((n * TOP_K, half), jnp.uint32),
        compiler_params=pltpu.CompilerParams(dimension_semantics=("arbitrary", "arbitrary"),
                                             vmem_limit_bytes=EXP_VMEM_LIMIT),
        name="routed_experts",
    )(be, bsub, bcnt, dst_slots, hp, w_gu, w_gu, w_down)


def _ffn_gu_kernel(x_ref, wg_ref, wu_ref, o_ref):
    x = x_ref[...].astype(BF16)
    gate = jnp.dot(x, wg_ref[...], preferred_element_type=F32)
    up = jnp.dot(x, wu_ref[...], preferred_element_type=F32)
    o_ref[...] = (_silu(gate) * up).astype(o_ref.dtype)


def ffn_gu(x, w_gu, tm, tn):
    n, d = x.shape
    hdim = w_gu.shape[1] // 2
    nj = hdim // tn
    return pl.pallas_call(
        _ffn_gu_kernel,
        grid=(pl.cdiv(n, tm), nj),
        in_specs=[pl.BlockSpec((tm, d), lambda i, j: (i, 0)),
                  pl.BlockSpec((d, tn), lambda i, j: (0, j)),
                  pl.BlockSpec((d, tn), lambda i, j: (0, nj + j))],
        out_specs=pl.BlockSpec((tm, tn), lambda i, j: (i, j)),
        out_shape=jax.ShapeDtypeStruct((n, hdim), BF16),
        compiler_params=_cparams(("arbitrary", "arbitrary")),
        name="ffn_gu",
    )(x, w_gu, w_gu)


def _combine_kernel(w_ref, ys_ref, sh_ref, x_ref, gt_ref, o_ref):
    tt = w_ref.shape[0]
    half = ys_ref.shape[1]
    lo_acc = sh_ref[:, :half]
    hi_acc = sh_ref[:, half:]
    for k in range(TOP_K):
        u = ys_ref[k * tt:(k + 1) * tt, :]
        wk = w_ref[:, k:k + 1]
        lo_acc = lo_acc + wk * lax.bitcast_convert_type(u << 16, F32)
        hi_acc = hi_acc + wk * lax.bitcast_convert_type(u & jnp.uint32(0xFFFF0000), F32)
    gt = gt_ref[0]
    o_ref[:, :half] = x_ref[:, :half] + gt[:, :half] * lo_acc
    o_ref[:, half:] = x_ref[:, half:] + gt[:, half:] * hi_acc


def moe_combine(wts, ys, shared, x, gt, row0, rows_per_group):
    m, d = x.shape
    tt = COMB_T
    rb0 = row0 // tt
    return pl.pallas_call(
        _combine_kernel,
        grid=(m // tt,),
        in_specs=[pl.BlockSpec((tt, TOP_K), lambda i: (rb0 + i, 0)),
                  pl.BlockSpec((TOP_K * tt, d // 2), lambda i: (rb0 + i, 0)),
                  pl.BlockSpec((tt, d), lambda i: (rb0 + i, 0)),
                  pl.BlockSpec((tt, d), lambda i: (i, 0)),
                  _row_mod_spec(gt, tt, rows_per_group)],
        out_specs=pl.BlockSpec((tt, d), lambda i: (i, 0)),
        out_shape=jax.ShapeDtypeStruct((m, d), F32),
        compiler_params=_cparams(("arbitrary",)),
        name="moe_combine",
    )(wts, ys, shared, x, gt)


def _token_mixing(l, z4, bsz, t_real, t_seq, pos, a_caches, ret_s0, wkv_s0, shift_row, p):
    za, zb, zc, _ = z4
    prompt = a_caches is None

    def seq(z, t_to):
        z = z.reshape(bsz, t_real, z.shape[-1])
        return z if t_to == t_real else jnp.pad(z, ((0, 0), (0, t_to - t_real), (0, 0)))

    def rows(o):
        return o[:, :t_real].reshape(bsz * t_real, o.shape[-1])

    pos_seq = pos[0] + jnp.arange(t_seq, dtype=jnp.int32)
    tabs_a = rope_tables_a(pos_seq)
    tabs_b = rope_tables_b(pos_seq)
    za3, zb3 = seq(za, t_seq), seq(zb, t_seq)

    a_rows = []
    if prompt:
        oa = attn_prompt(za3, tabs_a)
        for g, (win, _) in enumerate(A_PATTERNS):
            nrow = min(win, t_seq)
            a_rows.append(kv_rows(za3, tabs_a, g, nrow, 128).reshape(bsz, nrow, 2, H_A, HD_A))
    else:
        nr = t_seq * H_A
        tabs_rows = tuple(jnp.repeat(tb, H_A, axis=0) for tb in tabs_a)
        z6 = za3.reshape(bsz, t_seq, 3, 3, H_A, HD_A)
        state = (jnp.zeros((bsz, nr, HD_A), F32), jnp.zeros((bsz, nr, HD_A), F32),
                 jnp.full((bsz, nr, HD_A), -jnp.inf, F32))
        for g in range(3):
            qkv = [z6[:, :, g, j].reshape(bsz, nr, HD_A) for j in range(3)]
            state = attn_sample(qkv, a_caches[g], l, tabs_rows, state, g, g == 2)
            kv = kv_rows(za3, tabs_a, g, t_seq, t_seq).reshape(bsz, t_seq, 2, H_A, HD_A)
            a_rows.append(kv[:, :t_real])
        oa = state[0].reshape(bsz, t_seq, A_OUT).astype(BF16)

    lc = RET_CHUNK if prompt else t_seq
    ob, ret_s = retention(zb3, tabs_b, retention_tables(lc, lc if prompt else t_real), ret_s0, lc)

    t_wkv = t_seq if prompt else WKV_L
    tt = 256 if prompt else WKV_L
    zc3 = seq(zc, t_wkv)
    prep = wkv_prep(zc3, shift_row, p, tt, None if prompt else t_real)
    y, wkv_s = wkv_chunk(prep, wkv_s0)
    oc = wkv_post(y, prep[7], prep[8], p['lnx_w'], p['lnx_b'], tt)
    shift_new = zc3[:, t_real - 1, :C_COLS]
    return (rows(oa), rows(ob), rows(oc)), (a_rows[0], a_rows[1], a_rows[2], ret_s, wkv_s, shift_new)


def kernel(x_prompt, x_sample, c_prompt, c_sample, cache_a0, cache_a1, cache_a2, state_ret, state_wkv,
           state_shift, w_ada, b_ada, norm1, norm2, w_in, mu_c, w_decay0, w_decay_up, a0, a_up, g_up, k_k,
           k_a, r_k, lnx_w, lnx_b, w_up_a, w_up_b, w_up_c, w_out, w_router, b_router, w_exp_gu, w_exp_down,
           w_sh_gu, w_sh_down, norm_f):
    bp, tp, d = x_prompt.shape
    bs, ts, _ = x_sample.shape
    n_layers = w_ada.shape[0]
    mp, ms = bp * tp, bs * ts
    n_tok = mp + ms
    assert ms == COMB_T and mp % COMB_T == 0
    n_blocks = N_EXPERTS + (n_tok * TOP_K) // EXP_BMAX
    caches = (cache_a0, cache_a1, cache_a2)

    c_all = jnp.concatenate([c_prompt, c_sample, jnp.zeros((16 - bp - bs, d), F32)], axis=0)
    mod = ada_mod(c_all, w_ada, b_ada)

    xp = x_prompt.reshape(mp, d)
    xs = x_sample.reshape(ms, d)
    pos_p = jnp.arange(tp, dtype=jnp.int32)
    pos_s = PAST_LEN + jnp.arange(ts, dtype=jnp.int32)
    new_p = [[] for _ in range(6)]
    new_s = [[] for _ in range(6)]
    for l in range(n_layers):
        p = {'mu_c': mu_c[l], 'w_decay0': w_decay0[l], 'w_decay_up': w_decay_up[l], 'a0': a0[l], 'a_up': a_up[l],
             'g_up': g_up[l], 'k_k': k_k[l], 'k_a': k_a[l], 'r_k': r_k[l], 'lnx_w': lnx_w[l], 'lnx_b': lnx_b[l]}
        mods_p = [m.reshape(bp, 1, d) for m in jnp.split(mod[l, :bp], 6, axis=-1)]
        mods_s = [jnp.repeat(m, ts, axis=0).reshape(1, ms, d) for m in jnp.split(mod[l, bp:bp + bs], 6, axis=-1)]
        sh1p, sc1p, gt1p, sh2p, sc2p, gt2p = mods_p
        sh1s, sc1s, gt1s, sh2s, sc2s, gt2s = mods_s

        w_l = w_in[l]
        w_parts = (w_l[:, :A_COLS].astype(BF16),
                   w_l[:, A_COLS:A_COLS + B_COLS].astype(BF16),
                   jnp.pad(w_l[:, A_COLS + B_COLS:A_COLS + B_COLS + C_COLS], ((0, 0), (0, C_PAD - C_COLS))).astype(BF16),
                   w_l[:, A_COLS + B_COLS + C_COLS:].astype(BF16))
        tns = (1024, 1024, C_PAD // 3, 1024)
        hp = norm_mod(xp, norm1[l], sc1p, sh1p, 256, tp, BF16)
        hs = norm_mod(xs, norm1[l], sc1s, sh1s, ms, ms, BF16)
        z_p = [matmul(hp, w, 1024, tn) for w, tn in zip(w_parts, tns)]
        z_s = [matmul(hs, w, ms, tn) for w, tn in zip(w_parts, tns)]

        zeros_c = jnp.zeros((bp, 1, C_PAD), F32)
        o_p, st_p = _token_mixing(l, z_p, bp, tp, tp, pos_p, None,
                                  jnp.zeros((bp, H_B, DK_B, DV_B), F32), jnp.zeros((bp, H_C, HD_C, HD_C), F32),
                                  zeros_c, p)
        shift_s = jnp.pad(state_shift[l], ((0, 0), (0, C_PAD - C_COLS))).reshape(bs, 1, C_PAD)
        o_s, st_s = _token_mixing(l, z_s, bs, ts, SAMPLE_T_PAD, pos_s, caches, state_ret[l], state_wkv[l],
                                  shift_s, p)
        for lst, v in zip(new_p, st_p):
            lst.append(v)
        for lst, v in zip(new_s, st_s):
            lst.append(v)

        wa, wb, wc = w_up_a[l].astype(BF16), w_up_b[l].astype(BF16), w_up_c[l].astype(BF16)
        wo = w_out[l].astype(BF16)
        merged_p = upmerge(*o_p, wa, wb, wc, z_p[3], 1024, 512)
        merged_s = upmerge(*o_s, wa, wb, wc, z_s[3], ms, 512)
        x1p = proj_residual(merged_p, wo, xp, gt1p, 1024, 1024, tp)
        x1s = proj_residual(merged_s, wo, xs, gt1s, ms, 1024, ms)

        h2, h2_packed = norm_mod_moe(x1p, x1s, norm2[l], sc2p, sh2p, sc2s, sh2s, 256, tp)
        idx, wts, rank, counts = router(h2, w_router[l], b_router[l], 256)
        dst_slots, be, bsub, bcnt = route_tables(idx, rank, counts, n_blocks)
        ys = routed_experts(h2_packed, dst_slots, be, bsub, bcnt, w_exp_gu, w_exp_down, l)
        hid = ffn_gu(h2, w_sh_gu[l].astype(BF16), 1024, 256)
        shared = matmul(hid, w_sh_down[l].astype(BF16), 1024, 1024)
        xp = moe_combine(wts, ys, shared, x1p, gt2p, 0, tp)
        xs = moe_combine(wts, ys, shared, x1s, gt2s, mp, ms)

    y_prompt = final_norm(xp, norm_f, 256).reshape(bp, tp, d)
    y_sample = final_norm(xs, norm_f, ms).reshape(bs, ts, d)
    outs_p = [jnp.stack(z) for z in new_p]
    outs_s = [jnp.stack(z) for z in new_s]
    return (y_prompt, y_sample, *outs_p, *outs_s)
```

```python
import functools
import math

import jax
import jax.numpy as jnp
import numpy as np
from jax import lax
from jax.experimental import pallas as pl
from jax.experimental.pallas import tpu as pltpu

F32 = jnp.float32
BF16 = jnp.bfloat16

D_MODEL = 4096
PAST_LEN = 16384
A_PATTERNS = ((128, 1), (512, 4), (2048, 16))
H_A = 8
HD_A = 128
A_OUT = H_A * HD_A
ROT_A = HD_A // 4
ROPE_THETA = 500000.0
A_COLS = 3 * 3 * A_OUT
H_B = 4
DK_B = 128
DV_B = 256
B_OUT = H_B * DV_B
B_COLS = 2 * H_B * DK_B + 2 * B_OUT
RET_THETA = 10000.0
RET_CHUNK = 128
H_C = 16
HD_C = 64
C_W = H_C * HD_C
LORA_W = 64
LORA_A = 64
LORA_G = 160
C_COLS = 3 * C_W + LORA_W + LORA_A + LORA_G
C_PAD = 3456
LORA_PAD = C_PAD - 3 * C_W
WKV_GN_EPS = 64e-5
N_EXPERTS = 128
N_EXP_GROUPS = 8
TOPK_GROUPS = 4
TOP_K = 8
D_EXPERT = 768
ROUTED_SCALE = 2.5
NORM_EPS = 1e-6
SAMPLE_T_PAD = 8

VMEM_LIMIT = 56 * 1024 * 1024
WKV_L = 32
WKV_G = 4


def _cparams(sem):
    return pltpu.CompilerParams(dimension_semantics=sem, vmem_limit_bytes=VMEM_LIMIT)


def _bdot(a, b):
    return jnp.dot(a.astype(BF16), b.astype(BF16), preferred_element_type=F32)


def _bdot_t(a, b):
    return lax.dot_general(a.astype(BF16), b.astype(BF16), (((1,), (1,)), ((), ())),
                           preferred_element_type=F32)


def _split3(a):
    a1 = a.astype(BF16)
    r1 = a - a1.astype(F32)
    a2 = r1.astype(BF16)
    a3 = (r1 - a2.astype(F32)).astype(BF16)
    return a1, a2, a3


def _dot_general3(a, b, dims):
    a1, a2, a3 = _split3(a)
    b1, b2, b3 = _split3(b)
    dg = functools.partial(lax.dot_general, dimension_numbers=dims, preferred_element_type=F32)
    small = dg(a1, b3) + dg(a3, b1) + dg(a2, b2)
    mid = dg(a1, b2) + dg(a2, b1)
    return dg(a1, b1) + (mid + small)


_NN = (((1,), (0,)), ((), ()))
_NT = (((1,), (1,)), ((), ()))
_TN = (((0,), (0,)), ((), ()))


def _silu(x):
    return x * jax.nn.sigmoid(x)


def _ada_kernel(c_ref, w_ref, b_ref, o_ref):
    part = _bdot(_silu(c_ref[...]), w_ref[0])

    @pl.when(pl.program_id(2) == 0)
    def _first():
        o_ref[0] = part + b_ref[0]

    @pl.when(pl.program_id(2) > 0)
    def _rest():
        o_ref[0] += part


def ada_mod(c_all, w_ada, b_ada):
    n_layers, d, n = w_ada.shape
    rows = c_all.shape[0]
    tn, tk = 2048, 1024
    return pl.pallas_call(
        _ada_kernel,
        grid=(n_layers, n // tn, d // tk),
        in_specs=[pl.BlockSpec((rows, tk), lambda l, j, k: (0, k)),
                  pl.BlockSpec((1, tk, tn), lambda l, j, k: (l, k, j)),
                  pl.BlockSpec((1, 1, tn), lambda l, j, k: (l, 0, j))],
        out_specs=pl.BlockSpec((1, rows, tn), lambda l, j, k: (l, 0, j)),
        out_shape=jax.ShapeDtypeStruct((n_layers, rows, n), F32),
        compiler_params=_cparams(("arbitrary", "arbitrary", "arbitrary")),
        name="ada_mod",
    )(c_all, w_ada, b_ada.reshape(n_layers, 1, n))


def _row_mod_spec(mod, tile, rows_per_group, tn=None):
    per_group = mod.shape[1] == 1
    lead = (lambda i: (i * tile // rows_per_group, 0)) if per_group else (lambda i: (0, i))
    rows = 1 if per_group else tile
    if tn is None:
        return pl.BlockSpec((1, rows, mod.shape[2]), lambda i: (*lead(i), 0))
    return pl.BlockSpec((1, rows, tn), lambda i, j: (*lead(i), j))


def _norm_mod_kernel(x_ref, g_ref, sc_ref, sh_ref, o_ref):
    x = x_ref[...]
    y = x * lax.rsqrt(jnp.mean(x * x, axis=-1, keepdims=True) + NORM_EPS) * g_ref[...]
    o_ref[...] = (y * (1.0 + sc_ref[0]) + sh_ref[0]).astype(o_ref.dtype)


def norm_mod(x2, g, sc, sh, tile, rows_per_group, out_dtype):
    m, d = x2.shape
    spec = _row_mod_spec(sc, tile, rows_per_group)
    return pl.pallas_call(
        _norm_mod_kernel,
        grid=(m // tile,),
        in_specs=[pl.BlockSpec((tile, d), lambda i: (i, 0)), pl.BlockSpec((1, d), lambda i: (0, 0)), spec, spec],
        out_specs=pl.BlockSpec((tile, d), lambda i: (i, 0)),
        out_shape=jax.ShapeDtypeStruct((m, d), out_dtype),
        compiler_params=_cparams(("arbitrary",)),
        name="norm_mod",
    )(x2, g.reshape(1, d), sc, sh)


def _final_norm_kernel(x_ref, g_ref, o_ref):
    x = x_ref[...]
    o_ref[...] = x * lax.rsqrt(jnp.mean(x * x, axis=-1, keepdims=True) + NORM_EPS) * g_ref[...]


def final_norm(x2, g, tile):
    m, d = x2.shape
    return pl.pallas_call(
        _final_norm_kernel,
        grid=(m // tile,),
        in_specs=[pl.BlockSpec((tile, d), lambda i: (i, 0)), pl.BlockSpec((1, d), lambda i: (0, 0))],
        out_specs=pl.BlockSpec((tile, d), lambda i: (i, 0)),
        out_shape=jax.ShapeDtypeStruct((m, d), F32),
        compiler_params=_cparams(("arbitrary",)),
        name="final_norm",
    )(x2, g.reshape(1, d))


def _mm_kernel(x_ref, w_ref, o_ref):
    x = x_ref[...]
    if x.dtype == F32:
        x1, x2, x3 = _split3(x)
        dot = functools.partial(jnp.dot, preferred_element_type=F32)
        w = w_ref[...]
        o_ref[...] = (dot(x1, w) + (dot(x2, w) + dot(x3, w))).astype(o_ref.dtype)
    else:
        o_ref[...] = _bdot(x, w_ref[...]).astype(o_ref.dtype)


def matmul(x, w, tm, tn, out_dtype=F32, row0=0, rows=None):
    k = x.shape[1]
    rows = x.shape[0] - row0 if rows is None else rows
    n = w.shape[1]
    rb0 = row0 // tm
    return pl.pallas_call(
        _mm_kernel,
        grid=(pl.cdiv(rows, tm), n // tn),
        in_specs=[pl.BlockSpec((tm, k), lambda i, j: (rb0 + i, 0)),
                  pl.BlockSpec((k, tn), lambda i, j: (0, j))],
        out_specs=pl.BlockSpec((tm, tn), lambda i, j: (i, j)),
        out_shape=jax.ShapeDtypeStruct((rows, n), out_dtype),
        compiler_params=_cparams(("arbitrary", "arbitrary")),
        name="proj",
    )(x, w)


def rope_tables_a(pos):
    half = ROT_A // 2
    inv = ROPE_THETA ** (-jnp.arange(half, dtype=F32) / half)
    ang = pos.astype(F32)[:, None] * inv[None, :]
    cos, sin = jnp.cos(ang), jnp.sin(ang)
    t = pos.shape[0]
    one = jnp.ones((t, HD_A - ROT_A), F32)
    zero_tail = jnp.zeros((t, HD_A - ROT_A), F32)
    zero_h = jnp.zeros((t, half), F32)
    c = jnp.concatenate([cos, cos, one], axis=1)
    sp = jnp.concatenate([zero_h, sin, zero_tail], axis=1)
    sm = jnp.concatenate([-sin, zero_h, zero_tail], axis=1)
    return c, sp, sm


def rope_tables_b(pos):
    half = DK_B // 2
    inv = RET_THETA ** (-jnp.arange(half, dtype=F32) / half)
    ang = pos.astype(F32)[:, None] * inv[None, :]
    cos, sin = jnp.cos(ang), jnp.sin(ang)
    return jnp.concatenate([cos, cos], axis=1), jnp.concatenate([-sin, sin], axis=1)


def _rope_a(x, c, sp, sm):
    half = ROT_A // 2
    return x * c + pltpu.roll(x, half, 1) * sp + pltpu.roll(x, HD_A - half, 1) * sm


def _attn_prompt_kernel(q0, k0, v0, q1, k1, v1, q2, k2, v2, c_ref, sp_ref, sm_ref, o_ref,
                        qr, kr, og, mg, sg):
    t = qr.shape[0]
    blk = 128
    n_iter = t // blk
    qi = lax.broadcasted_iota(jnp.int32, (blk, blk), 0)
    ki = lax.broadcasted_iota(jnp.int32, (blk, blk), 1)
    cur_ok = qi >= ki
    prev_ok = ki >= qi
    neg = jnp.float32(-jnp.inf)
    scale = HD_A ** -0.5
    for g, (qf, kf, vf) in enumerate(((q0, k0, v0), (q1, k1, v1), (q2, k2, v2))):
        dil = A_PATTERNS[g][1]
        nb = t // (dil * blk)
        c, sp, sm = c_ref[...], sp_ref[...], sm_ref[...]
        qr[...] = _rope_a(qf[0], c, sp, sm) * scale
        kr[...] = _rope_a(kf[0], c, sp, sm)

        def body(it, carry, g=g, dil=dil, nb=nb, vf=vf):
            r = it % dil
            n = it // dil
            cur = pl.ds(n * (blk * dil) + r, blk, stride=dil)
            q = qr[cur, :]
            lc = jnp.where(cur_ok, _bdot_t(q, kr[cur, :]), neg)
            m = jnp.max(lc, axis=-1, keepdims=True)
            if nb > 1:
                prev = pl.ds(jnp.maximum(n - 1, 0) * (blk * dil) + r, blk, stride=dil)
                lp = jnp.where(prev_ok & (n > 0), _bdot_t(q, kr[prev, :]), neg)
                m = jnp.maximum(m, jnp.max(lp, axis=-1, keepdims=True))
            pc = jnp.exp(lc - m)
            s = jnp.sum(pc, axis=-1, keepdims=True)
            o = _bdot(pc, vf[0, cur, :])
            if nb > 1:
                pp = jnp.exp(lp - m)
                s = s + jnp.sum(pp, axis=-1, keepdims=True)
                o = o + _bdot(pp, vf[0, prev, :])
            og[g, cur, :] = o
            mg[g, cur, :] = jnp.broadcast_to(m, (blk, HD_A))
            sg[g, cur, :] = jnp.broadcast_to(s, (blk, HD_A))
            return carry

        lax.fori_loop(0, n_iter, body, 0, unroll=4)
    m_all = jnp.maximum(jnp.maximum(mg[0], mg[1]), mg[2])
    num = jnp.zeros_like(m_all)
    den = jnp.zeros_like(m_all)
    for g in range(3):
        e = jnp.exp(mg[g] - m_all)
        num = num + e * og[g]
        den = den + e * sg[g]
    o_ref[0] = (num / den).astype(o_ref.dtype)


def attn_prompt(za, tabs):
    b, t, _ = za.shape
    assert t % (A_PATTERNS[-1][1] * 128) == 0
    col_specs = []
    for g in range(3):
        for j in range(3):
            col_specs.append(pl.BlockSpec((1, t, HD_A), lambda bi, h, cb=(g * 3 + j) * H_A: (bi, 0, cb + h)))
    tab_spec = pl.BlockSpec((t, HD_A), lambda bi, h: (0, 0))
    return pl.pallas_call(
        _attn_prompt_kernel,
        grid=(b, H_A),
        in_specs=col_specs + [tab_spec] * 3,
        out_specs=pl.BlockSpec((1, t, HD_A), lambda bi, h: (bi, 0, h)),
        out_shape=jax.ShapeDtypeStruct((b, t, A_OUT), BF16),
        scratch_shapes=[pltpu.VMEM((t, HD_A), F32), pltpu.VMEM((t, HD_A), F32),
                        pltpu.VMEM((3, t, HD_A), F32), pltpu.VMEM((3, t, HD_A), F32),
                        pltpu.VMEM((3, t, HD_A), F32)],
        compiler_params=_cparams(("arbitrary", "arbitrary")),
        name="attn_prompt",
    )(*([za] * 9), *tabs)


def _kv_rows_kernel(k_ref, v_ref, c_ref, sp_ref, sm_ref, o_ref):
    rt = k_ref.shape[1]
    c, sp, sm = c_ref[...], sp_ref[...], sm_ref[...]
    for h in range(H_A):
        cols = slice(h * HD_A, (h + 1) * HD_A)
        o_ref[0, pl.ds(h, rt, stride=2 * H_A), :] = _rope_a(k_ref[0, :, cols], c, sp, sm)
        o_ref[0, pl.ds(H_A + h, rt, stride=2 * H_A), :] = v_ref[0, :, cols]


def kv_rows(za, tabs, g, rows, rt):
    b, t, _ = za.shape
    off = (t - rows) // rt
    tab_spec = pl.BlockSpec((rt, HD_A), lambda bi, i: (off + i, 0))
    return pl.pallas_call(
        _kv_rows_kernel,
        grid=(b, rows // rt),
        in_specs=[pl.BlockSpec((1, rt, A_OUT), lambda bi, i: (bi, off + i, g * 3 + 1)),
                  pl.BlockSpec((1, rt, A_OUT), lambda bi, i: (bi, off + i, g * 3 + 2)),
                  tab_spec, tab_spec, tab_spec],
        out_specs=pl.BlockSpec((1, rt * 2 * H_A, HD_A), lambda bi, i: (bi, i, 0)),
        out_shape=jax.ShapeDtypeStruct((b, rows * 2 * H_A, HD_A), F32),
        compiler_params=_cparams(("arbitrary", "arbitrary")),
        name="kv_rows",
    )(za, za, *tabs)


def _attn_sample_kernel(q_ref, k_ref, v_ref, cache_ref, c_ref, sp_ref, sm_ref, num_ref, den_ref, m_ref,
                        onum_ref, oden_ref, om_ref, *, dil, n_back, final):
    rows = q_ref.shape[1]
    lb = cache_ref.shape[2]
    c, sp, sm = c_ref[...], sp_ref[...], sm_ref[...]
    q = _rope_a(q_ref[0], c, sp, sm) * (HD_A ** -0.5)
    kn = _rope_a(k_ref[0], c, sp, sm)
    kc = cache_ref[0, 0, :, 0, :, :].reshape(lb * H_A, HD_A)
    vc = cache_ref[0, 0, :, 1, :, :].reshape(lb * H_A, HD_A)
    neg = jnp.float32(-jnp.inf)
    span = n_back * dil

    row = lax.broadcasted_iota(jnp.int32, (rows, lb * H_A), 0)
    col = lax.broadcasted_iota(jnp.int32, (rows, lb * H_A), 1)
    dist = lb + (row // H_A) - (col // H_A)
    ok = ((col % H_A) == (row % H_A)) & ((dist % dil) == 0) & (dist <= span)
    lc = jnp.where(ok, _bdot_t(q, kc), neg)

    rown = lax.broadcasted_iota(jnp.int32, (rows, rows), 0)
    coln = lax.broadcasted_iota(jnp.int32, (rows, rows), 1)
    dn = (rown // H_A) - (coln // H_A)
    okn = ((coln % H_A) == (rown % H_A)) & (dn >= 0) & ((dn % dil) == 0) & (dn <= span)
    ln = jnp.where(okn, _bdot_t(q, kn), neg)

    m = jnp.maximum(jnp.max(lc, axis=-1, keepdims=True), jnp.max(ln, axis=-1, keepdims=True))
    pc = jnp.exp(lc - m)
    pn = jnp.exp(ln - m)
    s = jnp.sum(pc, axis=-1, keepdims=True) + jnp.sum(pn, axis=-1, keepdims=True)
    o = _bdot(pc, vc) + _bdot(pn, v_ref[0])

    m_prev = m_ref[0]
    m_new = jnp.maximum(m_prev, m)
    e_prev = jnp.exp(m_prev - m_new)
    e_cur = jnp.exp(m - m_new)
    num = num_ref[0] * e_prev + o * e_cur
    den = den_ref[0] * e_prev + s * e_cur
    onum_ref[0] = num / den if final else num
    oden_ref[0] = den
    om_ref[0] = m_new


def attn_sample(qkv, cache, layer, tabs, state, g, final):
    b, rows, _ = qkv[0].shape
    lb = cache.shape[2]
    win, dil = A_PATTERNS[g]
    row_spec = pl.BlockSpec((1, rows, HD_A), lambda bi: (bi, 0, 0))
    tab_spec = pl.BlockSpec((rows, HD_A), lambda bi: (0, 0))
    st = jax.ShapeDtypeStruct((b, rows, HD_A), F32)
    return pl.pallas_call(
        functools.partial(_attn_sample_kernel, dil=dil, n_back=win // dil, final=final),
        grid=(b,),
        in_specs=[row_spec, row_spec, row_spec,
                  pl.BlockSpec((1, 1, lb, 2, H_A, HD_A), lambda bi: (layer, bi, 0, 0, 0, 0)),
                  tab_spec, tab_spec, tab_spec, row_spec, row_spec, row_spec],
        out_specs=[row_spec, row_spec, row_spec],
        out_shape=[st, st, st],
        compiler_params=_cparams(("arbitrary",)),
        name="attn_sample",
    )(*qkv, cache, *tabs, *state)


def retention_tables(lc, l_real):
    log_gamma = np.log(1.0 - 2.0 ** (-5.0 - np.arange(H_B, dtype=np.float32))).astype(np.float32)
    i = np.arange(lc, dtype=np.float32)
    diff = i[:, None] - i[None, :]
    real = (i < l_real)
    inner = np.where((diff >= 0) & real[None, :], np.exp(np.maximum(diff, 0.0)[None] * log_gamma[:, None, None]), 0.0)
    qdec = np.exp((i + 1.0)[None, :] * log_gamma[:, None])[:, :, None]
    kdec = np.where(real[None, :], np.exp((l_real - 1.0 - i)[None, :] * log_gamma[:, None]), 0.0)[:, :, None]
    gl = np.exp(l_real * log_gamma)[:, None, None]
    f = lambda z: jnp.asarray(z, F32)
    return f(inner), f(qdec), f(kdec), f(gl)


def _retention_kernel(q_ref, k_ref, v_ref, g_ref, c_ref, s_ref, dm_ref, qd_ref, kd_ref, gl_ref, s0_ref,
                      o_ref, so_ref, *, lc):
    t = q_ref.shape[1]
    so_ref[0, 0] = s0_ref[0, 0]
    dm = dm_ref[0]
    qd = qd_ref[0]
    kd = kd_ref[0]
    gl = gl_ref[0]

    def body(ci, carry):
        rows = pl.ds(pl.multiple_of(ci * lc, lc), lc)
        c, s = c_ref[rows, :], s_ref[rows, :]
        qx, kx = q_ref[0, rows, :], k_ref[0, rows, :]
        q = qx * c + pltpu.roll(qx, DK_B // 2, 1) * s
        k = (kx * c + pltpu.roll(kx, DK_B // 2, 1) * s) * (DK_B ** -0.5)
        v = v_ref[0, rows, :]
        st = so_ref[0, 0]
        o = _bdot(_bdot_t(q, k) * dm, v) + _bdot(q, st) * qd
        so_ref[0, 0] = st * gl + lax.dot_general((k * kd).astype(BF16), v.astype(BF16), _TN,
                                                 preferred_element_type=F32)
        o = o * lax.rsqrt(jnp.mean(o * o, axis=-1, keepdims=True) + NORM_EPS)
        o_ref[0, rows, :] = (_silu(g_ref[0, rows, :]) * o).astype(o_ref.dtype)
        return carry

    lax.fori_loop(0, t // lc, body, 0)


def retention(zb, tabs, dec, s0, lc):
    b, t, _ = zb.shape
    nqk = H_B * DK_B // DV_B
    tab_spec = pl.BlockSpec((t, DK_B), lambda bi, h: (0, 0))
    return pl.pallas_call(
        functools.partial(_retention_kernel, lc=lc),
        grid=(b, H_B),
        in_specs=[pl.BlockSpec((1, t, DK_B), lambda bi, h: (bi, 0, h)),
                  pl.BlockSpec((1, t, DK_B), lambda bi, h: (bi, 0, H_B + h)),
                  pl.BlockSpec((1, t, DV_B), lambda bi, h: (bi, 0, 2 * nqk + h)),
                  pl.BlockSpec((1, t, DV_B), lambda bi, h: (bi, 0, 2 * nqk + H_B + h)),
                  tab_spec, tab_spec,
                  pl.BlockSpec((1, lc, lc), lambda bi, h: (h, 0, 0)),
                  pl.BlockSpec((1, lc, 1), lambda bi, h: (h, 0, 0)),
                  pl.BlockSpec((1, lc, 1), lambda bi, h: (h, 0, 0)),
                  pl.BlockSpec((1, 1, 1), lambda bi, h: (h, 0, 0)),
                  pl.BlockSpec((1, 1, DK_B, DV_B), lambda bi, h: (bi, h, 0, 0))],
        out_specs=[pl.BlockSpec((1, t, DV_B), lambda bi, h: (bi, 0, h)),
                   pl.BlockSpec((1, 1, DK_B, DV_B), lambda bi, h: (bi, h, 0, 0))],
        out_shape=[jax.ShapeDtypeStruct((b, t, B_OUT), BF16),
                   jax.ShapeDtypeStruct((b, H_B, DK_B, DV_B), F32)],
        compiler_params=_cparams(("arbitrary", "arbitrary")),
        name="retention",
    )(zb, zb, zb, zb, *tabs, *dec, s0)


def _seg_sum(x, seg_ref):
    x1 = x.astype(BF16)
    x2 = (x - x1.astype(F32)).astype(BF16)
    seg = seg_ref[...]
    return jnp.dot(x1, seg, preferred_element_type=F32) + jnp.dot(x2, seg, preferred_element_type=F32)


def _wkv_prep_kernel(zc_ref, zp_ref, sh_ref, mu_ref, wl_ref, w0_ref, a0_ref, kk_ref, ka_ref, rk_ref,
                     seg_ref, tri_ref, ones_ref,
                     rt_ref, at_ref, bt_ref, kt_ref, bh_ref, kh_ref, v_ref, bon_ref, g_ref, wlast_ref,
                     *, t_valid):
    i = pl.program_id(1)
    tt = zc_ref.shape[1]
    zc = zc_ref[0]
    row = lax.broadcasted_iota(jnp.int32, (tt, 1), 0)
    first = jnp.where(i == 0, sh_ref[0], zp_ref[0, 7:8, :])
    prev = jnp.where(row == 0, first, pltpu.roll(zc, 1, 0))
    xc = zc + (prev - zc) * mu_ref[...]
    r = xc[:, 0:C_W]
    k = xc[:, C_W:2 * C_W]
    v = xc[:, 2 * C_W:3 * C_W]
    xl = xc[:, 3 * C_W:]
    lane = lax.broadcasted_iota(jnp.int32, xl.shape, 1)
    act = jnp.where(lane < LORA_W, jnp.tanh(xl),
                    jnp.where(lane < LORA_W + LORA_A, xl,
                              jnp.where(lane < LORA_W + LORA_A + LORA_G, jax.nn.sigmoid(xl), 0.0)))
    lora = _bdot(act, wl_ref[...])
    wl = w0_ref[...] + lora[:, 0:C_W]
    softplus = jnp.maximum(-wl, 0.0) + jnp.log(1.0 + jnp.exp(-jnp.abs(wl)))
    lw = -jnp.exp(-softplus - 0.5)
    a = jax.nn.sigmoid(a0_ref[...] + lora[:, C_W:2 * C_W])
    g_ref[0] = lora[:, 2 * C_W:3 * C_W]
    kk = k * kk_ref[...]
    kk = kk / jnp.maximum(jnp.sqrt(_seg_sum(kk * kk, seg_ref)), 1e-12)
    k2 = k * (1.0 + (a - 1.0) * ka_ref[...])
    bon_ref[0] = _seg_sum(r * k2 * rk_ref[...], seg_ref) * v
    al = -kk
    be = kk * a
    if t_valid is not None:
        ok = (i * tt + row) < t_valid
        lw = jnp.where(ok, lw, 0.0)
        al = jnp.where(ok, al, 0.0)
        be = jnp.where(ok, be, 0.0)
        k2 = jnp.where(ok, k2, 0.0)
    l1, l2, l3 = _split3(lw)
    tri = tri_ref[...]
    ones = ones_ref[...]
    dot = functools.partial(jnp.dot, preferred_element_type=F32)
    cum = dot(tri, l1) + (dot(tri, l2) + dot(tri, l3))
    tot = dot(ones, l1) + (dot(ones, l2) + dot(ones, l3))
    winv = jnp.exp(-cum)
    wrest = jnp.exp(tot - cum)
    rt_ref[0] = r * jnp.exp(cum)
    at_ref[0] = al * jnp.exp(cum - lw)
    bt_ref[0] = be * winv
    kt_ref[0] = k2 * winv
    bh_ref[0] = be * wrest
    kh_ref[0] = k2 * wrest
    v_ref[0] = v
    wl_all = jnp.exp(tot)
    for c in range(tt // WKV_L):
        wlast_ref[0, c:c + 1, :] = wl_all[c * WKV_L:c * WKV_L + 1, :]


def _blockdiag_ones(n, blk, lower):
    i = np.arange(n)
    same = (i[:, None] // blk) == (i[None, :] // blk)
    m = same & (i[:, None] >= i[None, :]) if lower else same
    return jnp.asarray(m, BF16)


def wkv_prep(zc, shift_row, p, tt, t_valid):
    b, t, _ = zc.shape
    row = lambda z: z.reshape(1, -1)
    mu = jnp.pad(p['mu_c'], (0, C_PAD - C_COLS)).reshape(1, C_PAD)
    w_lora = jnp.zeros((LORA_PAD, 3 * C_W), F32)
    w_lora = w_lora.at[0:LORA_W, 0:C_W].set(p['w_decay_up'])
    w_lora = w_lora.at[LORA_W:LORA_W + LORA_A, C_W:2 * C_W].set(p['a_up'])
    w_lora = w_lora.at[LORA_W + LORA_A:LORA_W + LORA_A + LORA_G, 2 * C_W:].set(p['g_up']).astype(BF16)
    seg = _blockdiag_ones(C_W, HD_C, False)
    tri = _blockdiag_ones(tt, WKV_L, True)
    ones = _blockdiag_ones(tt, WKV_L, False)
    vec = pl.BlockSpec((1, C_W), lambda bi, i: (0, 0))
    const = lambda shape: pl.BlockSpec(shape, lambda bi, i: (0, 0))
    blk = pl.BlockSpec((1, tt, C_W), lambda bi, i: (bi, i, 0))
    full = jax.ShapeDtypeStruct((b, t, C_W), F32)
    return pl.pallas_call(
        functools.partial(_wkv_prep_kernel, t_valid=t_valid),
        grid=(b, t // tt),
        in_specs=[pl.BlockSpec((1, tt, C_PAD), lambda bi, i: (bi, i, 0)),
                  pl.BlockSpec((1, 8, C_PAD), lambda bi, i: (bi, jnp.maximum(i * (tt // 8) - 1, 0), 0)),
                  pl.BlockSpec((1, 1, C_PAD), lambda bi, i: (bi, 0, 0)),
                  const((1, C_PAD)), const((LORA_PAD, 3 * C_W)), vec, vec, vec, vec, vec,
                  const((C_W, C_W)), const((tt, tt)), const((tt, tt))],
        out_specs=[blk] * 9 + [pl.BlockSpec((1, tt // WKV_L, C_W), lambda bi, i: (bi, i, 0))],
        out_shape=[full] * 9 + [jax.ShapeDtypeStruct((b, t // WKV_L, C_W), F32)],
        compiler_params=_cparams(("arbitrary", "arbitrary")),
        name="wkv_prep",
    )(zc, zc, shift_row, mu, w_lora, row(p['w_decay0']), row(p['a0']), row(p['k_k']), row(p['k_a']),
      row(p['r_k']), seg, tri, ones)


def _wkv_chunk_kernel(rt_ref, at_ref, bt_ref, kt_ref, bh_ref, kh_ref, v_ref, wl_ref, s0_ref,
                      mst_ref, pm_ref, eye_ref, y_ref, so_ref):
    tt = rt_ref.shape[1]
    n = WKV_G * WKV_L
    w = WKV_G * HD_C

    @pl.when(pl.program_id(1) == 0)
    def _init():
        so_ref[...] = s0_ref[...]

    mst = mst_ref[...]
    pm = pm_ref[...]
    eye = eye_ref[...]

    def stack(x):
        return (jnp.concatenate([x] * WKV_G, axis=0) * mst).astype(BF16)

    def body(ci, carry):
        rows = pl.ds(pl.multiple_of(ci * WKV_L, WKV_L), WKV_L)
        groups = range(H_C // WKV_G)
        lanes = [slice(hg * w, (hg + 1) * w) for hg in groups]
        ld = lambda ref, hg: stack(ref[0, rows, lanes[hg]])
        lhs = [jnp.concatenate([ld(at_ref, hg), ld(rt_ref, hg)], axis=0) for hg in groups]
        rhs = [jnp.concatenate([ld(bt_ref, hg), ld(kt_ref, hg)], axis=0) for hg in groups]
        vst = [ld(v_ref, hg) for hg in groups]
        s_bd = [so_ref[0, hg] for hg in groups]
        p = [_bdot_t(lhs[hg], rhs[hg]) * pm for hg in groups]
        q = [_bdot_t(lhs[hg], s_bd[hg]) for hg in groups]
        nil = [p[hg][:n, :n] for hg in groups]
        x = [eye + nil[hg] for hg in groups]
        for _ in range(int(math.log2(WKV_L)) - 1):
            nil = [_bdot(nil[hg], nil[hg]) for hg in groups]
            x = [x[hg] + _bdot(x[hg], nil[hg]) for hg in groups]
        rhs_u = [q[hg][:n] + _bdot(p[hg][:n, n:], vst[hg]) for hg in groups]
        u = [_bdot(x[hg], rhs_u[hg]) for hg in groups]
        uv = [jnp.concatenate([u[hg].astype(BF16), vst[hg]], axis=0) for hg in groups]
        y = [q[hg][n:] + _bdot(p[hg][n:], uv[hg]) for hg in groups]
        for hg in groups:
            acc = y[hg][0:WKV_L]
            for g in range(1, WKV_G):
                acc = acc + y[hg][g * WKV_L:(g + 1) * WKV_L]
            y_ref[0, rows, lanes[hg]] = acc
        hat = [jnp.concatenate([ld(bh_ref, hg), ld(kh_ref, hg)], axis=0) for hg in groups]
        upd = [lax.dot_general(uv[hg], hat[hg], _TN, preferred_element_type=F32) for hg in groups]
        for hg in groups:
            so_ref[0, hg] = s_bd[hg] * wl_ref[0, pl.ds(ci, 1), lanes[hg]] + upd[hg]
        return carry

    lax.fori_loop(0, tt // WKV_L, body, 0)


def wkv_chunk(prep, s0):
    rt, at, bt, kt, bh, kh, v, _, _, wlast = prep
    b, t, _ = rt.shape
    ng = H_C // WKV_G
    w = WKV_G * HD_C
    n = WKV_G * WKV_L
    s0g = s0.astype(F32).reshape(b, ng, WKV_G, HD_C, HD_C)
    eye_g = jnp.eye(WKV_G, dtype=F32)
    s_bd = jnp.einsum('bngij,gh->bngihj', s0g, eye_g).reshape(b, ng, w, w)
    mst = jnp.asarray((np.arange(n)[:, None] // WKV_L) == (np.arange(w)[None, :] // HD_C), F32)
    pi = np.arange(2 * n)
    pm_np = np.where((pi[:, None] < n), (pi[:, None] % n) > (pi[None, :] % n), (pi[:, None] % n) >= (pi[None, :] % n))
    pm = jnp.asarray(pm_np, F32)
    eye = jnp.eye(n, dtype=F32)
    tt = min(t, 512)
    blk = pl.BlockSpec((1, tt, C_W), lambda bi, ti: (bi, ti, 0))
    const = lambda shape: pl.BlockSpec(shape, lambda bi, ti: (0, 0))
    st_spec = pl.BlockSpec((1, ng, w, w), lambda bi, ti: (bi, 0, 0, 0))
    y, s_new = pl.pallas_call(
        _wkv_chunk_kernel,
        grid=(b, t // tt),
        in_specs=[blk] * 7 + [pl.BlockSpec((1, tt // WKV_L, C_W), lambda bi, ti: (bi, ti, 0)), st_spec,
                              const((n, w)), const((2 * n, 2 * n)), const((n, n))],
        out_specs=[blk, st_spec],
        out_shape=[jax.ShapeDtypeStruct((b, t, C_W), F32), jax.ShapeDtypeStruct((b, ng, w, w), F32)],
        compiler_params=_cparams(("arbitrary", "arbitrary")),
        name="wkv_chunk",
    )(rt, at, bt, kt, bh, kh, v, wlast, s_bd, mst, pm, eye)
    s_new = jnp.einsum('bngihj,gh->bngij', s_new.reshape(b, ng, WKV_G, HD_C, WKV_G, HD_C), eye_g)
    return y, s_new.reshape(b, H_C, HD_C, HD_C)


def _wkv_post_kernel(y_ref, bon_ref, g_ref, lw_ref, lb_ref, seg_ref, o_ref):
    y = y_ref[0]
    mean = _seg_sum(y, seg_ref) * (1.0 / HD_C)
    d = y - mean
    var = _seg_sum(d * d, seg_ref) * (1.0 / HD_C)
    yn = d * lax.rsqrt(var + WKV_GN_EPS) * lw_ref[...] + lb_ref[...]
    o_ref[0] = ((yn + bon_ref[0]) * g_ref[0]).astype(o_ref.dtype)


def wkv_post(y, bon, g, lnx_w, lnx_b, tt):
    b, t, _ = y.shape
    blk = pl.BlockSpec((1, tt, C_W), lambda bi, i: (bi, i, 0))
    vec = pl.BlockSpec((1, C_W), lambda bi, i: (0, 0))
    return pl.pallas_call(
        _wkv_post_kernel,
        grid=(b, t // tt),
        in_specs=[blk, blk, blk, vec, vec, pl.BlockSpec((C_W, C_W), lambda bi, i: (0, 0))],
        out_specs=blk,
        out_shape=jax.ShapeDtypeStruct((b, t, C_W), BF16),
        compiler_params=_cparams(("arbitrary", "arbitrary")),
        name="wkv_post",
    )(y, bon, g, lnx_w.reshape(1, C_W), lnx_b.reshape(1, C_W), _blockdiag_ones(C_W, HD_C, False))


def _upmerge_kernel(oa_ref, ob_ref, oc_ref, wa_ref, wb_ref, wc_ref, g0_ref, g1_ref, g2_ref, o_ref):
    acc = jax.nn.sigmoid(g0_ref[...]) * jnp.dot(oa_ref[...], wa_ref[...], preferred_element_type=F32)
    acc = acc + jax.nn.sigmoid(g1_ref[...]) * jnp.dot(ob_ref[...], wb_ref[...], preferred_element_type=F32)
    acc = acc + jax.nn.sigmoid(g2_ref[...]) * jnp.dot(oc_ref[...], wc_ref[...], preferred_element_type=F32)
    o_ref[...] = acc.astype(o_ref.dtype)


def upmerge(oa, ob, oc, wa, wb, wc, zg, tm, tn):
    m = oa.shape[0]
    d = wa.shape[1]
    nj = d // tn
    act = pl.BlockSpec((tm, oa.shape[1]), lambda i, j: (i, 0))
    wsp = pl.BlockSpec((wa.shape[0], tn), lambda i, j: (0, j))
    gate = lambda br: pl.BlockSpec((tm, tn), lambda i, j: (i, br * nj + j))
    return pl.pallas_call(
        _upmerge_kernel,
        grid=(m // tm, nj),
        in_specs=[act, act, act, wsp, wsp, wsp, gate(0), gate(1), gate(2)],
        out_specs=pl.BlockSpec((tm, tn), lambda i, j: (i, j)),
        out_shape=jax.ShapeDtypeStruct((m, d), BF16),
        compiler_params=_cparams(("arbitrary", "arbitrary")),
        name="upmerge",
    )(oa, ob, oc, wa, wb, wc, zg, zg, zg)


def _proj_res_kernel(a_ref, w_ref, x_ref, gt_ref, o_ref):
    o_ref[...] = x_ref[...] + gt_ref[0] * _bdot(a_ref[...], w_ref[...])


def proj_residual(a, w, x, gt, tm, tn, rows_per_group):
    m, k = a.shape
    n = w.shape[1]
    return pl.pallas_call(
        _proj_res_kernel,
        grid=(m // tm, n // tn),
        in_specs=[pl.BlockSpec((tm, k), lambda i, j: (i, 0)),
                  pl.BlockSpec((k, tn), lambda i, j: (0, j)),
                  pl.BlockSpec((tm, tn), lambda i, j: (i, j)),
                  _row_mod_spec(gt, tm, rows_per_group, tn)],
        out_specs=pl.BlockSpec((tm, tn), lambda i, j: (i, j)),
        out_shape=jax.ShapeDtypeStruct((m, n), F32),
        compiler_params=_cparams(("arbitrary", "arbitrary")),
        name="proj_residual",
    )(a, w, x, gt)


EXP_SUB = 128
EXP_NSUB = 5
EXP_BMAX = EXP_SUB * EXP_NSUB
EXP_CE = 256
EXP_NC = D_EXPERT // EXP_CE
EXP_KSLABS = 4
EXP_JSLABS = 4
COMB_T = 32
COMB_ROWS = COMB_T * TOP_K


def _router_kernel(h_ref, w_ref, b_ref, tri_ref, idx_ref, wts_ref, rank_ref, cnt_ref, carry, *, n_valid):
    h1, h2, _ = _split3(h_ref[...])
    w1, w2, _ = _split3(w_ref[...])
    dot = functools.partial(jnp.dot, preferred_element_type=F32)
    logits = dot(h1, w1) + (dot(h1, w2) + dot(h2, w1))
    scores = jax.nn.sigmoid(logits)
    choice = scores + b_ref[...]
    tm = choice.shape[0]
    lane = lax.broadcasted_iota(jnp.int32, (tm, N_EXPERTS), 1)
    per_group = N_EXPERTS // N_EXP_GROUPS
    grp = lane // per_group
    neg = jnp.float32(-jnp.inf)

    def first_argmax(x, ids, sentinel):
        m = jnp.max(x, axis=-1, keepdims=True)
        return m, jnp.min(jnp.where(x == m, ids, sentinel), axis=-1, keepdims=True)

    gscore = jnp.zeros_like(choice)
    for g in range(N_EXP_GROUPS):
        xg = jnp.where(grp == g, choice, neg)
        m1, i1 = first_argmax(xg, lane, N_EXPERTS)
        m2 = jnp.max(jnp.where(lane == i1, neg, xg), axis=-1, keepdims=True)
        gscore = jnp.where(grp == g, m1 + m2, gscore)
    keep = jnp.zeros(choice.shape, jnp.bool_)
    for _ in range(TOPK_GROUPS):
        _, gi = first_argmax(gscore, grp, N_EXP_GROUPS)
        keep = keep | (grp == gi)
        gscore = jnp.where(grp == gi, neg, gscore)
    masked = jnp.where(keep, choice, neg)
    lane_k = lax.broadcasted_iota(jnp.int32, (tm, TOP_K), 1)
    idx = jnp.zeros((tm, TOP_K), jnp.int32)
    wts = jnp.zeros((tm, TOP_K), F32)
    hits = []
    for k in range(TOP_K):
        _, ik = first_argmax(masked, lane, N_EXPERTS)
        hit = lane == ik
        hits.append(hit)
        wk = jnp.sum(jnp.where(hit, scores, 0.0), axis=-1, keepdims=True)
        masked = jnp.where(hit, neg, masked)
        idx = jnp.where(lane_k == k, ik, idx)
        wts = jnp.where(lane_k == k, wk, wts)
    wts_ref[...] = wts / jnp.sum(wts, axis=-1, keepdims=True) * ROUTED_SCALE
    idx_ref[...] = idx

    @pl.when(pl.program_id(0) == 0)
    def _init():
        carry[...] = jnp.zeros_like(carry)

    row = pl.program_id(0) * tm + lax.broadcasted_iota(jnp.int32, (tm, 1), 0)
    sel = jnp.zeros(choice.shape, F32)
    for hit in hits:
        sel = sel + jnp.where(hit, 1.0, 0.0)
    sel = jnp.where(row < n_valid, sel, 0.0)
    before = jnp.dot(tri_ref[...], sel.astype(BF16), preferred_element_type=F32) + carry[...]
    rank = jnp.zeros((tm, TOP_K), jnp.int32)
    for k, hit in enumerate(hits):
        rk = jnp.sum(jnp.where(hit, before, 0.0), axis=-1, keepdims=True)
        rank = jnp.where(lane_k == k, rk.astype(jnp.int32), rank)
    rank_ref[...] = rank
    carry[...] = carry[...] + jnp.sum(sel, axis=0, keepdims=True)
    cnt_ref[...] = carry[...].astype(jnp.int32)


def router(h, w_router, b_router, tm):
    n, d = h.shape
    tri = jnp.asarray(np.arange(tm)[:, None] > np.arange(tm)[None, :], BF16)
    tok = pl.BlockSpec((tm, TOP_K), lambda i: (i, 0))
    return pl.pallas_call(
        functools.partial(_router_kernel, n_valid=n),
        grid=(pl.cdiv(n, tm),),
        in_specs=[pl.BlockSpec((tm, d), lambda i: (i, 0)),
                  pl.BlockSpec((d, N_EXPERTS), lambda i: (0, 0)),
                  pl.BlockSpec((1, N_EXPERTS), lambda i: (0, 0)),
                  pl.BlockSpec((tm, tm), lambda i: (0, 0))],
        out_specs=[tok, tok, tok, pl.BlockSpec((1, N_EXPERTS), lambda i: (0, 0))],
        out_shape=[jax.ShapeDtypeStruct((n, TOP_K), jnp.int32), jax.ShapeDtypeStruct((n, TOP_K), F32),
                   jax.ShapeDtypeStruct((n, TOP_K), jnp.int32), jax.ShapeDtypeStruct((1, N_EXPERTS), jnp.int32)],
        scratch_shapes=[pltpu.VMEM((1, N_EXPERTS), F32)],
        compiler_params=_cparams(("arbitrary",)),
        name="router",
    )(h, w_router, b_router.reshape(1, N_EXPERTS), tri)


def _pack_bf16_pair(y):
    half = y.shape[1] // 2
    bits = lax.bitcast_convert_type(y.astype(BF16).astype(F32), jnp.uint32)
    return (bits[:, :half] >> 16) | (bits[:, half:] & jnp.uint32(0xFFFF0000))


def _unpack_bf16_pair(u):
    lo = lax.bitcast_convert_type(u << 16, F32).astype(BF16)
    hi = lax.bitcast_convert_type(u & jnp.uint32(0xFFFF0000), F32).astype(BF16)
    return lo, hi


def _norm_mod_moe_kernel(xp_ref, xs_ref, g_ref, scp_ref, shp_ref, scs_ref, shs_ref, h_ref, hp_ref, *, n_tiles):
    i = pl.program_id(0)

    def emit(x, sc, sh):
        rows = x.shape[0]
        y = x * lax.rsqrt(jnp.mean(x * x, axis=-1, keepdims=True) + NORM_EPS) * g_ref[...]
        y = y * (1.0 + sc) + sh
        h_ref[0:rows, :] = y
        hp_ref[0:rows, :] = _pack_bf16_pair(y)

    @pl.when(i < n_tiles)
    def _prompt():
        emit(xp_ref[...], scp_ref[0], shp_ref[0])

    @pl.when(i == n_tiles)
    def _sample():
        emit(xs_ref[...], scs_ref[0], shs_ref[0])


def norm_mod_moe(xp, xs, g, scp, shp, scs, shs, tile, rows_per_group):
    mp, d = xp.shape
    ms = xs.shape[0]
    n_tiles = mp // tile
    clamp = lambda i: jnp.minimum(i, n_tiles - 1)
    pspec = pl.BlockSpec((1, 1, d), lambda i: (clamp(i) * tile // rows_per_group, 0, 0))
    sspec = pl.BlockSpec((1, ms, d), lambda i: (0, 0, 0))
    return pl.pallas_call(
        functools.partial(_norm_mod_moe_kernel, n_tiles=n_tiles),
        grid=(n_tiles + 1,),
        in_specs=[pl.BlockSpec((tile, d), lambda i: (clamp(i), 0)),
                  pl.BlockSpec((ms, d), lambda i: (0, 0)),
                  pl.BlockSpec((1, d), lambda i: (0, 0)), pspec, pspec, sspec, sspec],
        out_specs=[pl.BlockSpec((tile, d), lambda i: (i, 0)), pl.BlockSpec((tile, d // 2), lambda i: (i, 0))],
        out_shape=[jax.ShapeDtypeStruct((mp + ms, d), F32), jax.ShapeDtypeStruct((mp + ms, d // 2), jnp.uint32)],
        compiler_params=_cparams(("arbitrary",)),
        name="norm_mod_moe",
    )(xp, xs, g.reshape(1, d), scp, shp, scs, shs)


def _slot_kernel(idx_ref, rank_ref, start_ref, o_ref):
    idx = idx_ref[...]
    tm = idx.shape[0]
    lane = lax.broadcasted_iota(jnp.int32, (tm, N_EXPERTS), 1)
    lane_k = lax.broadcasted_iota(jnp.int32, (tm, TOP_K), 1)
    start = start_ref[...].astype(F32)
    out = rank_ref[...]
    for k in range(TOP_K):
        hit = lane == idx[:, k:k + 1]
        s_k = jnp.sum(jnp.where(hit, start, 0.0), axis=-1, keepdims=True).astype(jnp.int32)
        out = out + jnp.where(lane_k == k, s_k, 0)
    o_ref[...] = out


def slot_rows(idx, rank, pad_start, tm):
    n = idx.shape[0]
    tok = pl.BlockSpec((tm, TOP_K), lambda i: (i, 0))
    return pl.pallas_call(
        _slot_kernel,
        grid=(pl.cdiv(n, tm),),
        in_specs=[tok, tok, pl.BlockSpec((1, N_EXPERTS), lambda i: (0, 0))],
        out_specs=tok,
        out_shape=jax.ShapeDtypeStruct((n, TOP_K), jnp.int32),
        compiler_params=_cparams(("arbitrary",)),
        name="slot_rows",
    )(idx, rank, pad_start.reshape(1, N_EXPERTS).astype(jnp.int32))


def route_tables(idx, rank, counts, n_blocks):
    n = idx.shape[0]
    counts = counts.reshape(N_EXPERTS)
    padded = (counts + EXP_SUB - 1) // EXP_SUB * EXP_SUB
    pad_end = jnp.cumsum(padded)
    pad_start = pad_end - padded
    slot_of = slot_rows(idx, rank, pad_start, 1024)
    tok = jnp.arange(n, dtype=jnp.int32)[:, None]
    kk = jnp.arange(TOP_K, dtype=jnp.int32)[None, :]
    dst_of = (tok // COMB_T) * COMB_ROWS + kk * COMB_T + tok % COMB_T
    n_sub = (n * TOP_K) // EXP_SUB + N_EXPERTS + EXP_NSUB - 1
    dst_slots = jnp.zeros((n_sub * EXP_SUB,), jnp.int32).at[slot_of.reshape(-1)].set(dst_of.reshape(-1))
    nblk = (counts + EXP_BMAX - 1) // EXP_BMAX
    blk_end = jnp.cumsum(nblk)
    bid = jnp.arange(n_blocks, dtype=jnp.int32)
    used = bid < blk_end[-1]
    be = jnp.sum((bid[:, None] >= blk_end[None, :]).astype(jnp.int32), axis=1)
    be = jnp.minimum(be, N_EXPERTS - 1)
    sel = be[:, None] == jnp.arange(N_EXPERTS, dtype=jnp.int32)[None, :]
    pick = lambda v: jnp.sum(jnp.where(sel, v[None, :], 0), axis=1)
    within = bid - pick(blk_end - nblk)
    bcnt = jnp.where(used, jnp.clip(pick(counts) - within * EXP_BMAX, 0, EXP_BMAX), 0).astype(jnp.int32)
    bsub = jnp.where(used, (pick(pad_start) + within * EXP_BMAX) // EXP_SUB, 0).astype(jnp.int32)
    last_e = jnp.sum(jnp.where(bid == blk_end[-1] - 1, be, 0))
    be = jnp.where(used, be, last_e).astype(jnp.int32)
    return dst_slots.reshape(n_sub, EXP_SUB), be, bsub, bcnt


ROW_UNROLL = 8


def _for_rows(n_rows, fn):
    n_main = n_rows // ROW_UNROLL

    def group(i, carry):
        for j in range(ROW_UNROLL):
            fn(i * ROW_UNROLL + j)
        return carry

    def single(r, carry):
        fn(r)
        return carry

    lax.fori_loop(0, n_main, group, 0)
    lax.fori_loop(n_main * ROW_UNROLL, n_rows, single, 0)


def _expert_kernel(be_ref, bsub_ref, bcnt_ref, dst_hbm, hp_hbm, wg_ref, wu_ref, wd_ref, ys_hbm,
                   dst_smem, xu, xb, wgu_b, wd_b, acc, sem_dst, sem_in, sem_out):
    vb = pl.program_id(0)
    c = pl.program_id(1)
    nb = pl.num_programs(0)
    cnt = bcnt_ref[vb]
    prev_cnt = jnp.where(vb > 0, bcnt_ref[jnp.maximum(vb - 1, 0)], 0)
    nxt = jnp.minimum(vb + 1, nb - 1)
    next_cnt = jnp.where(vb + 1 < nb, bcnt_ref[nxt], 0)
    cur = vb % 2
    half = xu.shape[1]

    def dst_copy(block, buf):
        return pltpu.make_async_copy(dst_hbm.at[pl.ds(bsub_ref[block], EXP_NSUB)], dst_smem.at[buf], sem_dst)

    sub_bits = EXP_SUB.bit_length() - 1
    t_bits = COMB_T.bit_length() - 1
    rows_bits = COMB_ROWS.bit_length() - 1

    def dst_at(buf, r):
        return dst_smem[buf, r >> sub_bits, r & (EXP_SUB - 1)]

    def gather_start(buf, n_rows):
        def issue(r):
            d = dst_at(buf, r)
            tok = ((d >> rows_bits) << t_bits) | (d & (COMB_T - 1))
            pltpu.make_async_copy(hp_hbm.at[pl.ds(tok, 1)], xu.at[pl.ds(r, 1)], sem_in).start()
        _for_rows(n_rows, issue)

    def gather_wait(n_rows):
        _for_rows(n_rows, lambda r: pltpu.make_async_copy(
            hp_hbm.at[pl.ds(0, 1)], xu.at[pl.ds(0, 1)], sem_in).wait())

    def scatter_wait(n_rows):
        _for_rows(n_rows, lambda r: pltpu.make_async_copy(
            acc.at[pl.ds(0, 1)], ys_hbm.at[pl.ds(0, 1)], sem_out).wait())

    @pl.when((vb == 0) & (c == 0))
    def _prologue():
        xu[...] = jnp.zeros_like(xu)
        cp = dst_copy(0, 0)
        cp.start()
        cp.wait()
        gather_start(0, cnt)

    @pl.when(cnt > 0)
    def _block():
        @pl.when(c == 0)
        def _arrive():
            gather_wait(cnt)
            lo, hi = _unpack_bf16_pair(xu[...])
            xb[:, :half] = lo
            xb[:, half:] = hi
            scatter_wait(prev_cnt)

            @pl.when(next_cnt > 0)
            def _():
                dst_copy(nxt, 1 - cur).start()

        @pl.when((c == 1) & (next_cnt > 0))
        def _prefetch():
            dst_copy(nxt, 1 - cur).wait()
            gather_start(1 - cur, next_cnt)

        n_k = wgu_b.shape[0]
        kw = wgu_b.shape[1]
        n_j = wd_b.shape[0]
        jw = wd_b.shape[2]

        def cast_gu(kh):
            rows = slice(kh * kw, (kh + 1) * kw)
            wgu_b[kh, :, :EXP_CE] = wg_ref[0, 0, rows, :].astype(BF16)
            wgu_b[kh, :, EXP_CE:] = wu_ref[0, 0, rows, :].astype(BF16)

        def cast_down(j):
            wd_b[j] = wd_ref[0, 0, :, j * jw:(j + 1) * jw].astype(BF16)

        cast_gu(0)
        hid = None
        for kh in range(n_k):
            if kh + 1 < n_k:
                cast_gu(kh + 1)
            else:
                cast_down(0)
            part = jnp.dot(xb[:, kh * kw:(kh + 1) * kw], wgu_b[kh], preferred_element_type=F32)
            hid = part if hid is None else hid + part
        act = (_silu(hid[:, :EXP_CE]) * hid[:, EXP_CE:]).astype(BF16)
        first = c == 0
        for j in range(n_j):
            if j + 1 < n_j:
                cast_down(j + 1)
            cols = slice(j * jw, (j + 1) * jw)
            yj = jnp.dot(act, wd_b[j], preferred_element_type=F32)
            acc[:, cols] = jnp.where(first, yj, acc[:, cols] + yj)

        @pl.when(c == EXP_NC - 1)
        def _write():
            _for_rows(cnt, lambda r: pltpu.make_async_copy(
                acc.at[pl.ds(r, 1)], ys_hbm.at[pl.ds(dst_at(cur, r), 1)], sem_out).start())

            @pl.when(next_cnt == 0)
            def _():
                scatter_wait(cnt)


def routed_experts(hp, dst_slots, be, bsub, bcnt, w_gu, w_down, layer):
    n, half = hp.shape
    d = 2 * half
    n_blocks = be.shape[0]

    def chunk(vb, c, bc_r):
        return jnp.where(bc_r[vb] > 0, c, EXP_NC - 1)

    grid_spec = pltpu.PrefetchScalarGridSpec(
        num_scalar_prefetch=3,
        grid=(n_blocks, EXP_NC),
        in_specs=[pl.BlockSpec(memory_space=pl.ANY),
                  pl.BlockSpec(memory_space=pl.ANY),
                  pl.BlockSpec((1, 1, d, EXP_CE), lambda vb, c, be_r, bs_r, bc_r: (layer, be_r[vb], 0, chunk(vb, c, bc_r))),
                  pl.BlockSpec((1, 1, d, EXP_CE), lambda vb, c, be_r, bs_r, bc_r: (layer, be_r[vb], 0, EXP_NC + chunk(vb, c, bc_r))),
                  pl.BlockSpec((1, 1, EXP_CE, d), lambda vb, c, be_r, bs_r, bc_r: (layer, be_r[vb], chunk(vb, c, bc_r), 0))],
        out_specs=pl.BlockSpec(memory_space=pl.ANY),
        scratch_shapes=[pltpu.SMEM((2, EXP_NSUB, EXP_SUB), jnp.int32),
                        pltpu.VMEM((EXP_BMAX, half), jnp.uint32),
                        pltpu.VMEM((EXP_BMAX, d), BF16),
                        pltpu.VMEM((EXP_KSLABS, d // EXP_KSLABS, 2 * EXP_CE), BF16),
                        pltpu.VMEM((EXP_JSLABS, EXP_CE, d // EXP_JSLABS), BF16),
                        pltpu.VMEM((EXP_BMAX, d), F32),
                        pltpu.SemaphoreType.DMA(()), pltpu.SemaphoreType.DMA(()), pltpu.SemaphoreType.DMA(())])
    return pl.pallas_call(
        _expert_kernel,
        grid_spec=grid_spec,
        out_shape=jax.ShapeDtypeStruct((n * TOP_K, d), F32),
        compiler_params=_cparams(("arbitrary", "arbitrary")),
        name="routed_experts",
    )(be, bsub, bcnt, dst_slots, hp, w_gu, w_gu, w_down)


def _ffn_gu_kernel(x_ref, wg_ref, wu_ref, o_ref):
    x = x_ref[...].astype(BF16)
    gate = jnp.dot(x, wg_ref[...], preferred_element_type=F32)
    up = jnp.dot(x, wu_ref[...], preferred_element_type=F32)
    o_ref[...] = (_silu(gate) * up).astype(o_ref.dtype)


def ffn_gu(x, w_gu, tm, tn):
    n, d = x.shape
    hdim = w_gu.shape[1] // 2
    nj = hdim // tn
    return pl.pallas_call(
        _ffn_gu_kernel,
        grid=(pl.cdiv(n, tm), nj),
        in_specs=[pl.BlockSpec((tm, d), lambda i, j: (i, 0)),
                  pl.BlockSpec((d, tn), lambda i, j: (0, j)),
                  pl.BlockSpec((d, tn), lambda i, j: (0, nj + j))],
        out_specs=pl.BlockSpec((tm, tn), lambda i, j: (i, j)),
        out_shape=jax.ShapeDtypeStruct((n, hdim), BF16),
        compiler_params=_cparams(("arbitrary", "arbitrary")),
        name="ffn_gu",
    )(x, w_gu, w_gu)


def _combine_kernel(w_ref, ys_ref, sh_ref, x_ref, gt_ref, o_ref):
    tt = w_ref.shape[0]
    acc = sh_ref[...]
    for k in range(TOP_K):
        acc = acc + w_ref[:, k:k + 1] * ys_ref[k * tt:(k + 1) * tt, :]
    o_ref[...] = x_ref[...] + gt_ref[0] * acc


def moe_combine(wts, ys, shared, x, gt, row0, rows_per_group):
    m, d = x.shape
    tt = COMB_T
    rb0 = row0 // tt
    return pl.pallas_call(
        _combine_kernel,
        grid=(m // tt,),
        in_specs=[pl.BlockSpec((tt, TOP_K), lambda i: (rb0 + i, 0)),
                  pl.BlockSpec((TOP_K * tt, d), lambda i: (rb0 + i, 0)),
                  pl.BlockSpec((tt, d), lambda i: (rb0 + i, 0)),
                  pl.BlockSpec((tt, d), lambda i: (i, 0)),
                  _row_mod_spec(gt, tt, rows_per_group)],
        out_specs=pl.BlockSpec((tt, d), lambda i: (i, 0)),
        out_shape=jax.ShapeDtypeStruct((m, d), F32),
        compiler_params=_cparams(("arbitrary",)),
        name="moe_combine",
    )(wts, ys, shared, x, gt)


def _token_mixing(l, z4, bsz, t_real, t_seq, pos, a_caches, ret_s0, wkv_s0, shift_row, p):
    za, zb, zc, _ = z4
    prompt = a_caches is None

    def seq(z, t_to):
        z = z.reshape(bsz, t_real, z.shape[-1])
        return z if t_to == t_real else jnp.pad(z, ((0, 0), (0, t_to - t_real), (0, 0)))

    def rows(o):
        return o[:, :t_real].reshape(bsz * t_real, o.shape[-1])

    pos_seq = pos[0] + jnp.arange(t_seq, dtype=jnp.int32)
    tabs_a = rope_tables_a(pos_seq)
    tabs_b = rope_tables_b(pos_seq)
    za3, zb3 = seq(za, t_seq), seq(zb, t_seq)

    a_rows = []
    if prompt:
        oa = attn_prompt(za3, tabs_a)
        for g, (win, _) in enumerate(A_PATTERNS):
            nrow = min(win, t_seq)
            a_rows.append(kv_rows(za3, tabs_a, g, nrow, 128).reshape(bsz, nrow, 2, H_A, HD_A))
    else:
        nr = t_seq * H_A
        tabs_rows = tuple(jnp.repeat(tb, H_A, axis=0) for tb in tabs_a)
        z6 = za3.reshape(bsz, t_seq, 3, 3, H_A, HD_A)
        state = (jnp.zeros((bsz, nr, HD_A), F32), jnp.zeros((bsz, nr, HD_A), F32),
                 jnp.full((bsz, nr, HD_A), -jnp.inf, F32))
        for g in range(3):
            qkv = [z6[:, :, g, j].reshape(bsz, nr, HD_A) for j in range(3)]
            state = attn_sample(qkv, a_caches[g], l, tabs_rows, state, g, g == 2)
            kv = kv_rows(za3, tabs_a, g, t_seq, t_seq).reshape(bsz, t_seq, 2, H_A, HD_A)
            a_rows.append(kv[:, :t_real])
        oa = state[0].reshape(bsz, t_seq, A_OUT).astype(BF16)

    lc = RET_CHUNK if prompt else t_seq
    ob, ret_s = retention(zb3, tabs_b, retention_tables(lc, lc if prompt else t_real), ret_s0, lc)

    t_wkv = t_seq if prompt else WKV_L
    tt = 256 if prompt else WKV_L
    zc3 = seq(zc, t_wkv)
    prep = wkv_prep(zc3, shift_row, p, tt, None if prompt else t_real)
    y, wkv_s = wkv_chunk(prep, wkv_s0)
    oc = wkv_post(y, prep[7], prep[8], p['lnx_w'], p['lnx_b'], tt)
    shift_new = zc3[:, t_real - 1, :C_COLS]
    return (rows(oa), rows(ob), rows(oc)), (a_rows[0], a_rows[1], a_rows[2], ret_s, wkv_s, shift_new)


def kernel(x_prompt, x_sample, c_prompt, c_sample, cache_a0, cache_a1, cache_a2, state_ret, state_wkv,
           state_shift, w_ada, b_ada, norm1, norm2, w_in, mu_c, w_decay0, w_decay_up, a0, a_up, g_up, k_k,
           k_a, r_k, lnx_w, lnx_b, w_up_a, w_up_b, w_up_c, w_out, w_router, b_router, w_exp_gu, w_exp_down,
           w_sh_gu, w_sh_down, norm_f):
    bp, tp, d = x_prompt.shape
    bs, ts, _ = x_sample.shape
    n_layers = w_ada.shape[0]
    mp, ms = bp * tp, bs * ts
    n_tok = mp + ms
    assert ms == COMB_T and mp % COMB_T == 0
    n_blocks = N_EXPERTS + (n_tok * TOP_K) // EXP_BMAX
    caches = (cache_a0, cache_a1, cache_a2)

    c_all = jnp.concatenate([c_prompt, c_sample, jnp.zeros((16 - bp - bs, d), F32)], axis=0)
    mod = ada_mod(c_all, w_ada, b_ada)

    xp = x_prompt.reshape(mp, d)
    xs = x_sample.reshape(ms, d)
    pos_p = jnp.arange(tp, dtype=jnp.int32)
    pos_s = PAST_LEN + jnp.arange(ts, dtype=jnp.int32)
    new_p = [[] for _ in range(6)]
    new_s = [[] for _ in range(6)]
    for l in range(n_layers):
        p = {'mu_c': mu_c[l], 'w_decay0': w_decay0[l], 'w_decay_up': w_decay_up[l], 'a0': a0[l], 'a_up': a_up[l],
             'g_up': g_up[l], 'k_k': k_k[l], 'k_a': k_a[l], 'r_k': r_k[l], 'lnx_w': lnx_w[l], 'lnx_b': lnx_b[l]}
        mods_p = [m.reshape(bp, 1, d) for m in jnp.split(mod[l, :bp], 6, axis=-1)]
        mods_s = [jnp.repeat(m, ts, axis=0).reshape(1, ms, d) for m in jnp.split(mod[l, bp:bp + bs], 6, axis=-1)]
        sh1p, sc1p, gt1p, sh2p, sc2p, gt2p = mods_p
        sh1s, sc1s, gt1s, sh2s, sc2s, gt2s = mods_s

        w_l = w_in[l]
        w_parts = (w_l[:, :A_COLS].astype(BF16),
                   w_l[:, A_COLS:A_COLS + B_COLS].astype(BF16),
                   jnp.pad(w_l[:, A_COLS + B_COLS:A_COLS + B_COLS + C_COLS], ((0, 0), (0, C_PAD - C_COLS))).astype(BF16),
                   w_l[:, A_COLS + B_COLS + C_COLS:].astype(BF16))
        tns = (1024, 1024, C_PAD // 3, 1024)
        hp = norm_mod(xp, norm1[l], sc1p, sh1p, 256, tp, BF16)
        hs = norm_mod(xs, norm1[l], sc1s, sh1s, ms, ms, F32)
        z_p = [matmul(hp, w, 1024, tn) for w, tn in zip(w_parts, tns)]
        z_s = [matmul(hs, w, ms, tn) for w, tn in zip(w_parts, tns)]

        zeros_c = jnp.zeros((bp, 1, C_PAD), F32)
        o_p, st_p = _token_mixing(l, z_p, bp, tp, tp, pos_p, None,
                                  jnp.zeros((bp, H_B, DK_B, DV_B), F32), jnp.zeros((bp, H_C, HD_C, HD_C), F32),
                                  zeros_c, p)
        shift_s = jnp.pad(state_shift[l], ((0, 0), (0, C_PAD - C_COLS))).reshape(bs, 1, C_PAD)
        o_s, st_s = _token_mixing(l, z_s, bs, ts, SAMPLE_T_PAD, pos_s, caches, state_ret[l], state_wkv[l],
                                  shift_s, p)
        for lst, v in zip(new_p, st_p):
            lst.append(v)
        for lst, v in zip(new_s, st_s):
            lst.append(v)

        wa, wb, wc = w_up_a[l].astype(BF16), w_up_b[l].astype(BF16), w_up_c[l].astype(BF16)
        wo = w_out[l].astype(BF16)
        merged_p = upmerge(*o_p, wa, wb, wc, z_p[3], 1024, 512)
        merged_s = upmerge(*o_s, wa, wb, wc, z_s[3], ms, 512)
        x1p = proj_residual(merged_p, wo, xp, gt1p, 1024, 1024, tp)
        x1s = proj_residual(merged_s, wo, xs, gt1s, ms, 1024, ms)

        h2, h2_packed = norm_mod_moe(x1p, x1s, norm2[l], sc2p, sh2p, sc2s, sh2s, 256, tp)
        idx, wts, rank, counts = router(h2, w_router[l], b_router[l], 256)
        dst_slots, be, bsub, bcnt = route_tables(idx, rank, counts, n_blocks)
        ys = routed_experts(h2_packed, dst_slots, be, bsub, bcnt, w_exp_gu, w_exp_down, l)
        hid = ffn_gu(h2, w_sh_gu[l].astype(BF16), 1024, 256)
        shared = matmul(hid, w_sh_down[l].astype(BF16), 1024, 1024)
        xp = moe_combine(wts, ys, shared, x1p, gt2p, 0, tp)
        xs = moe_combine(wts, ys, shared, x1s, gt2s, mp, ms)

    y_prompt = final_norm(xp, norm_f, 256).reshape(bp, tp, d)
    y_sample = final_norm(xs, norm_f, ms).reshape(bs, ts, d)
    outs_p = [jnp.stack(z) for z in new_p]
    outs_s = [jnp.stack(z) for z in new_s]
    return (y_prompt, y_sample, *outs_p, *outs_s)
```
